```python
import jax, jax.numpy as jnp
from jax import lax
import numpy as np

D_MODEL = 1024
BATCH = 16
SEQ = 4096
DEPTH = 1

PLE_DIM = 256
CHUNK = 128
RET_HEADS = 4
RET_V_DIM = D_MODEL // RET_HEADS
RET_QK_DIM = RET_V_DIM // 2
RET_WIDTH = RET_HEADS * RET_V_DIM
SGU_GROUPS = 4
SGU_WIDTH = D_MODEL
SGU_GROUP_DIM = SGU_WIDTH // SGU_GROUPS
ROPE_BASE = 10000.0
NORM_EPS = 1e-6
GN_EPS = 1e-5
IN_SPLITS = (RET_HEADS * RET_QK_DIM, RET_HEADS * RET_QK_DIM, RET_WIDTH, RET_WIDTH,
             SGU_WIDTH, SGU_WIDTH, SGU_WIDTH, D_MODEL, D_MODEL)
IN_WIDTH = sum(IN_SPLITS)

kernel_name = 'hybrid_retention_sgu_block'


def rms_norm(x, g):
    xf = x.astype(jnp.float32)
    y = xf * lax.rsqrt(jnp.mean(xf * xf, axis=-1, keepdims=True) + NORM_EPS)
    return (y * g.astype(jnp.float32)).astype(x.dtype)


def unit_norm(x, eps):
    xf = x.astype(jnp.float32)
    mu = jnp.mean(xf, axis=-1, keepdims=True)
    var = jnp.mean(jnp.square(xf - mu), axis=-1, keepdims=True)
    return ((xf - mu) * lax.rsqrt(var + eps)).astype(x.dtype)


def rotary(x):
    s, d = x.shape[1], x.shape[-1]
    half = d // 2
    inv = ROPE_BASE ** (-jnp.arange(half, dtype=jnp.float32) / half)
    ang = jnp.arange(s, dtype=jnp.float32)[:, None] * inv[None, :]
    cos = jnp.cos(ang)[None, :, None, :].astype(x.dtype)
    sin = jnp.sin(ang)[None, :, None, :].astype(x.dtype)
    x1, x2 = x[..., :half], x[..., half:]
    return jnp.concatenate([x1 * cos - x2 * sin, x2 * cos + x1 * sin], axis=-1)


def retention(q, k, v):
    b, s, h, dk = q.shape
    dv = v.shape[-1]
    n = s // CHUNK
    log_g = jnp.log(1.0 - 2.0 ** (-5.0 - jnp.arange(h, dtype=jnp.float32)))
    idx = jnp.arange(CHUNK, dtype=jnp.float32)
    diff = idx[:, None] - idx[None, :]
    decay_in = jnp.where(diff[None] >= 0,
                         jnp.exp(jnp.maximum(diff, 0.0)[None] * log_g[:, None, None]),
                         0.0).astype(q.dtype)
    zeta = jnp.exp((CHUNK - 1.0 - idx)[:, None] * log_g[None, :]).astype(q.dtype)
    xi = jnp.exp((idx + 1.0)[:, None] * log_g[None, :]).astype(q.dtype)
    chunk_decay = jnp.exp(CHUNK * log_g).astype(v.dtype)

    qc = (q * (dk ** -0.5)).reshape(b, n, CHUNK, h, dk)
    kc = k.reshape(b, n, CHUNK, h, dk)
    vc = v.reshape(b, n, CHUNK, h, dv)

    scores = jnp.einsum('bnihd,bnjhd->bnhij', qc, kc) * decay_in
    inner = jnp.einsum('bnhij,bnjhe->bnihe', scores, vc)

    kv = jnp.einsum('bnjhd,bnjhe->nbhde', kc * zeta[:, :, None], vc)

    def step(state, kv_n):
        return kv_n + chunk_decay[None, :, None, None] * state, state

    _, prev = lax.scan(step, jnp.zeros_like(kv[0]), kv)
    cross = jnp.einsum('bnihd,nbhde->bnihe', qc * xi[:, :, None], prev)
    return (inner + cross).reshape(b, s, h, dv)


def spatial_gating(u, v, ws, bs):
    b, s, _ = u.shape
    n = s // CHUNK
    v = unit_norm(v, GN_EPS).reshape(b, n, CHUNK, SGU_GROUPS, SGU_GROUP_DIM)
    ws_causal = ws * jnp.tril(jnp.ones((CHUNK, CHUNK), ws.dtype))[None]
    mixed = jnp.einsum('gij,bnjgd->bnigd', ws_causal, v) + bs.T[None, None, :, :, None]
    return u * mixed.reshape(b, s, SGU_WIDTH)


def setup_inputs(seed: int = 0) -> dict:
    key = jax.random.key(seed)
    ks = jax.random.split(key, 13)
    f32 = jnp.float32
    nrm = lambda k, shape, scale: jax.random.normal(k, shape, f32) * scale
    return {
        'x': jax.random.normal(ks[0], (BATCH, SEQ, D_MODEL), f32),
        'p': jax.random.normal(ks[1], (DEPTH, BATCH, SEQ, PLE_DIM), f32),
        'w_in': nrm(ks[2], (DEPTH, D_MODEL, IN_WIDTH), D_MODEL ** -0.5),
        'w_ret_out': nrm(ks[3], (DEPTH, RET_WIDTH, D_MODEL), RET_WIDTH ** -0.5),
        'w_sgu_out': nrm(ks[4], (DEPTH, SGU_WIDTH, D_MODEL), SGU_WIDTH ** -0.5),
        'w_out': nrm(ks[5], (DEPTH, D_MODEL, D_MODEL), D_MODEL ** -0.5),
        'sgu_ws': nrm(ks[6], (DEPTH, SGU_GROUPS, CHUNK, CHUNK), CHUNK ** -0.5),
        'sgu_bs': 1.0 + nrm(ks[7], (DEPTH, SGU_GROUPS, CHUNK), 0.01),
        'w_ple_gate': nrm(ks[8], (DEPTH, D_MODEL, D_MODEL), D_MODEL ** -0.5),
        'w_ple_proj': nrm(ks[9], (DEPTH, PLE_DIM, D_MODEL), PLE_DIM ** -0.5),
        'g_mixer': 1.0 + nrm(ks[10], (DEPTH, D_MODEL), 0.05),
        'g_ple': 1.0 + nrm(ks[11], (DEPTH, D_MODEL), 0.05),
        'g_final': 1.0 + nrm(ks[12], (D_MODEL,), 0.05),
    }


def reference(x, p, w_in, w_ret_out, w_sgu_out, w_out, sgu_ws, sgu_bs,
              w_ple_gate, w_ple_proj, g_mixer, g_ple, g_final):
    b, s, _ = x.shape
    split_at = np.cumsum(IN_SPLITS)[:-1].tolist()
    for i in range(DEPTH):
        h = rms_norm(x, g_mixer[i])
        proj = jnp.einsum('bsd,de->bse', h, w_in[i])
        q, k, v, ret_gate, su, sv, sgu_gate, merge_ret, merge_sgu = jnp.split(proj, split_at, axis=-1)

        q = rotary(q.reshape(b, s, RET_HEADS, RET_QK_DIM))
        k = rotary(k.reshape(b, s, RET_HEADS, RET_QK_DIM))
        ret = retention(q, k, v.reshape(b, s, RET_HEADS, RET_V_DIM))
        ret = unit_norm(ret, GN_EPS).reshape(b, s, RET_WIDTH) * jax.nn.silu(ret_gate)

        sgu = spatial_gating(jax.nn.gelu(su, approximate=False), jax.nn.gelu(sv, approximate=False),
                             sgu_ws[i], sgu_bs[i]) * jax.nn.silu(sgu_gate)

        merged = (jax.nn.sigmoid(merge_ret) * jnp.einsum('bse,ed->bsd', ret, w_ret_out[i])
                  + jax.nn.sigmoid(merge_sgu) * jnp.einsum('bse,ed->bsd', sgu, w_sgu_out[i]))
        x = x + jnp.einsum('bsd,de->bse', merged, w_out[i])

        ple_gate = jax.nn.sigmoid(jnp.einsum('bsd,de->bse', rms_norm(x, g_ple[i]), w_ple_gate[i]))
        x = x + ple_gate * jnp.einsum('bsp,pd->bsd', p[i], w_ple_proj[i])
    return rms_norm(x, g_final)
```

```python
import functools

import jax
import jax.numpy as jnp
from jax import lax
from jax.experimental import pallas as pl
from jax.experimental.pallas import tpu as pltpu

D_MODEL = 1024
PLE_DIM = 256
CHUNK = 128
RET_HEADS = 4
RET_V_DIM = D_MODEL // RET_HEADS
RET_QK_DIM = RET_V_DIM // 2
SGU_GROUPS = 4
SGU_GROUP_DIM = D_MODEL // SGU_GROUPS
ROPE_BASE = 10000.0
NORM_EPS = 1e-6
GN_EPS = 1e-5

_QK0 = 0
_V0 = 2 * RET_HEADS * RET_QK_DIM
_RG0 = _V0 + D_MODEL
_SU0 = _RG0 + D_MODEL
_SV0 = _SU0 + D_MODEL
_SG0 = _SV0 + D_MODEL
_MR0 = _SG0 + D_MODEL
_MS0 = _MR0 + D_MODEL
IN_WIDTH = _MS0 + D_MODEL

TOKENS_PER_STEP = 256
V7X_VMEM_LIMIT_BYTES = 56 * 1024 * 1024

_F32 = jnp.float32
_BF16 = jnp.bfloat16


def _dot(a, b):
    return jnp.dot(a, b, preferred_element_type=_F32)


def _rms(x, g):
    ms = jnp.mean(x * x, axis=-1, keepdims=True)
    return x * lax.rsqrt(ms + NORM_EPS) * g


def _unit(x, eps):
    mu = jnp.mean(x, axis=-1, keepdims=True)
    d = x - mu
    var = jnp.mean(d * d, axis=-1, keepdims=True)
    return d * lax.rsqrt(var + eps)


def _silu(x):
    return x * jax.nn.sigmoid(x)


def _gelu(x):
    return 0.5 * x * (1.0 + lax.erf(x * (0.5 ** 0.5)))


def _block_kernel(x_ref, p_ref, cq_ref, sq_ref, ck_ref, sk_ref, xi_ref, zeta_ref,
                  decay_ref, cdec_ref, win_ref, wro_ref, wso_ref, wo_ref, wsc_ref,
                  bs_ref, wpg_ref, wpp_ref, gm_ref, gp_ref, gf_ref,
                  o_ref,
                  state_ref, h_s, q_s, qx_s, k_s, kz_s, v_s, rn_s, vn_s, mix_s,
                  ret_s, sgu_s):
    ts = x_ref.shape[1]
    n_chunks = ts // CHUNK
    dk, dv, gd = RET_QK_DIM, RET_V_DIM, SGU_GROUP_DIM

    @pl.when(pl.program_id(1) == 0)
    def _():
        state_ref[...] = jnp.zeros_like(state_ref)

    x = x_ref[0]
    h_s[...] = _rms(x, gm_ref[...]).astype(_BF16)

    qk = _dot(h_s[...], win_ref[:, _QK0:_V0])
    cq, sq, ck, sk = cq_ref[...], sq_ref[...], ck_ref[...], sk_ref[...]
    for hd in range(RET_HEADS):
        lanes = slice(hd * dk, (hd + 1) * dk)
        qh = qk[:, hd * dk:(hd + 1) * dk]
        qr = qh * cq + pltpu.roll(qh, dk // 2, 1) * sq
        q_s[:, lanes] = qr.astype(_BF16)
        qx_s[:, lanes] = (qr * xi_ref[hd]).astype(_BF16)
        kh = qk[:, (RET_HEADS + hd) * dk:(RET_HEADS + hd + 1) * dk]
        kr = kh * ck + pltpu.roll(kh, dk // 2, 1) * sk
        k_s[:, lanes] = kr.astype(_BF16)
        kz_s[:, lanes] = (kr * zeta_ref[hd]).astype(_BF16)

    v_s[...] = _dot(h_s[...], win_ref[:, _V0:_RG0]).astype(_BF16)

    for c in range(n_chunks):
        rows = slice(c * CHUNK, (c + 1) * CHUNK)
        for hd in range(RET_HEADS):
            lanes = slice(hd * dk, (hd + 1) * dk)
            vl = slice(hd * dv, (hd + 1) * dv)
            v = v_s[rows, vl]
            scores = lax.dot_general(q_s[rows, lanes], k_s[rows, lanes],
                                     (((1,), (1,)), ((), ())),
                                     preferred_element_type=_F32) * decay_ref[hd]
            st = state_ref[hd]
            r = _dot(scores.astype(_BF16), v) + _dot(qx_s[rows, lanes], st.astype(_BF16))
            kv = lax.dot_general(kz_s[rows, lanes], v, (((0,), (0,)), ((), ())),
                                 preferred_element_type=_F32)
            state_ref[hd] = kv + cdec_ref[hd] * st
            rn_s[rows, vl] = _unit(r, GN_EPS)

    rg = _dot(h_s[...], win_ref[:, _RG0:_SU0])
    ret_s[...] = (rn_s[...] * _silu(rg)).astype(_BF16)

    sv = _dot(h_s[...], win_ref[:, _SV0:_SG0])
    vn_s[...] = _unit(_gelu(sv), GN_EPS).astype(_BF16)
    for c in range(n_chunks):
        rows = slice(c * CHUNK, (c + 1) * CHUNK)
        for g in range(SGU_GROUPS):
            gl = slice(g * gd, (g + 1) * gd)
            mix_s[rows, gl] = _dot(wsc_ref[g], vn_s[rows, gl]) + bs_ref[g]
    su = _dot(h_s[...], win_ref[:, _SU0:_SV0])
    sg = _dot(h_s[...], win_ref[:, _SG0:_MR0])
    sgu_s[...] = (_gelu(su) * mix_s[...] * _silu(sg)).astype(_BF16)

    mr = jax.nn.sigmoid(_dot(h_s[...], win_ref[:, _MR0:_MS0]))
    ms = jax.nn.sigmoid(_dot(h_s[...], win_ref[:, _MS0:IN_WIDTH]))
    merged = mr * _dot(ret_s[...], wro_ref[...]) + ms * _dot(sgu_s[...], wso_ref[...])
    x1 = x_ref[0] + _dot(merged.astype(_BF16), wo_ref[...])
    hp = _rms(x1, gp_ref[...]).astype(_BF16)
    pg = jax.nn.sigmoid(_dot(hp, wpg_ref[...]))
    pp = _dot(p_ref[0].astype(_BF16), wpp_ref[...])
    x2 = x1 + pg * pp
    o_ref[0] = _rms(x2, gf_ref[...])


def _position_tables(seq, ts):
    dk = RET_QK_DIM
    half = dk // 2
    inv = ROPE_BASE ** (-jnp.arange(half, dtype=_F32) / half)
    ang = jnp.arange(seq, dtype=_F32)[:, None] * inv[None, :]
    cos, sin = jnp.cos(ang), jnp.sin(ang)
    cos_full = jnp.concatenate([cos, cos], axis=-1)
    sin_signed = jnp.concatenate([-sin, sin], axis=-1)
    scale = dk ** -0.5

    log_g = jnp.log(1.0 - 2.0 ** (-5.0 - jnp.arange(RET_HEADS, dtype=_F32)))
    idx = jnp.arange(CHUNK, dtype=_F32)
    diff = idx[:, None] - idx[None, :]
    decay = jnp.where(diff[None] >= 0,
                      jnp.exp(jnp.maximum(diff, 0.0)[None] * log_g[:, None, None]), 0.0)
    zeta = jnp.exp((CHUNK - 1.0 - idx)[None, :] * log_g[:, None])
    xi = jnp.exp((idx + 1.0)[None, :] * log_g[:, None])
    cdec = jnp.exp(CHUNK * log_g)
    reps = ts // CHUNK
    tile = lambda t: jnp.broadcast_to(jnp.tile(t, (1, reps))[:, :, None],
                                      (RET_HEADS, ts, dk))
    return (cos_full * scale, sin_signed * scale, cos_full, sin_signed,
            tile(xi), tile(zeta), decay,
            jnp.broadcast_to(cdec[:, None, None], (RET_HEADS, 1, RET_V_DIM)))


def _const_spec(shape):
    zeros = (0,) * len(shape)
    return pl.BlockSpec(shape, lambda b, t: zeros, pipeline_mode=pl.Buffered(1))


@jax.jit
def kernel(x, p, w_in, w_ret_out, w_sgu_out, w_out, sgu_ws, sgu_bs, w_ple_gate,
           w_ple_proj, g_mixer, g_ple, g_final):
    batch, seq, d = x.shape
    depth = w_in.shape[0]
    ts = TOKENS_PER_STEP
    assert d == D_MODEL and seq % ts == 0 and ts % CHUNK == 0
    cq, sq, ck, sk, xi, zeta, decay, cdec = _position_tables(seq, ts)
    tril = jnp.tril(jnp.ones((CHUNK, CHUNK), _F32))

    pos_spec = pl.BlockSpec((ts, RET_QK_DIM), lambda b, t: (t, 0))
    tok_spec = lambda width: pl.BlockSpec((1, ts, width), lambda b, t: (b, t, 0))

    for i in range(depth):
        ws_causal = (sgu_ws[i] * tril[None]).astype(_BF16)
        bs_rows = jnp.broadcast_to(sgu_bs[i][:, :, None],
                                   (SGU_GROUPS, CHUNK, SGU_GROUP_DIM)).astype(_F32)
        consts = (xi, zeta, decay, cdec,
                  w_in[i].astype(_BF16), w_ret_out[i].astype(_BF16),
                  w_sgu_out[i].astype(_BF16), w_out[i].astype(_BF16),
                  ws_causal, bs_rows,
                  w_ple_gate[i].astype(_BF16), w_ple_proj[i].astype(_BF16),
                  g_mixer[i][None, :], g_ple[i][None, :],
                  g_final[None, :])
        assert depth == 1
        act = lambda width, dt: pltpu.VMEM((ts, width), dt)
        x = pl.pallas_call(
            _block_kernel,
            grid=(batch, seq // ts),
            in_specs=[tok_spec(D_MODEL), tok_spec(PLE_DIM),
                      pos_spec, pos_spec, pos_spec, pos_spec]
                     + [_const_spec(c.shape) for c in consts],
            out_specs=tok_spec(D_MODEL),
            out_shape=jax.ShapeDtypeStruct(x.shape, x.dtype),
            scratch_shapes=[
                pltpu.VMEM((RET_HEADS, RET_QK_DIM, RET_V_DIM), _F32),
                act(D_MODEL, _BF16),
                act(RET_HEADS * RET_QK_DIM, _BF16),
                act(RET_HEADS * RET_QK_DIM, _BF16),
                act(RET_HEADS * RET_QK_DIM, _BF16),
                act(RET_HEADS * RET_QK_DIM, _BF16),
                act(D_MODEL, _BF16),
                act(D_MODEL, _F32),
                act(D_MODEL, _BF16),
                act(D_MODEL, _F32),
                act(D_MODEL, _BF16),
                act(D_MODEL, _BF16),
            ],
            compiler_params=pltpu.CompilerParams(
                dimension_semantics=("arbitrary", "arbitrary"),
                vmem_limit_bytes=V7X_VMEM_LIMIT_BYTES),
            name="hybrid_block",
        )(x, p[i], cq, sq, ck, sk, *consts)
    return x
```

```python
import functools

import jax
import jax.numpy as jnp
from jax import lax
from jax.experimental import pallas as pl
from jax.experimental.pallas import tpu as pltpu

D_MODEL = 1024
PLE_DIM = 256
CHUNK = 128
RET_HEADS = 4
RET_V_DIM = D_MODEL // RET_HEADS
RET_QK_DIM = RET_V_DIM // 2
SGU_GROUPS = 4
SGU_GROUP_DIM = D_MODEL // SGU_GROUPS
ROPE_BASE = 10000.0
NORM_EPS = 1e-6
GN_EPS = 1e-5

_QK0 = 0
_V0 = 2 * RET_HEADS * RET_QK_DIM
_RG0 = _V0 + D_MODEL
_SU0 = _RG0 + D_MODEL
_SV0 = _SU0 + D_MODEL
_SG0 = _SV0 + D_MODEL
_MR0 = _SG0 + D_MODEL
_MS0 = _MR0 + D_MODEL
IN_WIDTH = _MS0 + D_MODEL

TOKENS_PER_STEP = 256
V7X_VMEM_LIMIT_BYTES = 56 * 1024 * 1024

_F32 = jnp.float32
_BF16 = jnp.bfloat16


def _dot(a, b):
    return jnp.dot(a, b, preferred_element_type=_F32)


def _pack_rows(w):
    k, n = w.shape
    pairs = jnp.swapaxes(w.astype(_BF16).reshape(k // 2, 2, n), -1, -2)
    return lax.bitcast_convert_type(pairs, jnp.uint32)


def _wdot(a, w_packed):
    return _dot(a, pltpu.bitcast(w_packed, _BF16))


def _rms(x, g):
    ms = jnp.mean(x * x, axis=-1, keepdims=True)
    return x * lax.rsqrt(ms + NORM_EPS) * g


def _unit(x, eps):
    mu = jnp.mean(x, axis=-1, keepdims=True)
    d = x - mu
    var = jnp.mean(d * d, axis=-1, keepdims=True)
    return d * lax.rsqrt(var + eps)


def _silu(x):
    return x * jax.nn.sigmoid(x)


def _gelu(x):
    return 0.5 * x * (1.0 + lax.erf(x * (0.5 ** 0.5)))


def _block_kernel(tiles_per_seq,
                  x_ref, xl_ref, pl_ref, cq_ref, sq_ref, ck_ref, sk_ref, xi_ref, zeta_ref,
                  decay_ref, cdec_ref, win_ref, wro_ref, wso_ref, wo_ref, wsc_ref,
                  bs_ref, wpg_ref, wpp_ref, gm_ref, gp_ref, gf_ref,
                  o_ref,
                  state_ref, h_s, q_s, qx_s, k_s, kz_s, v_s, gate_s, ret_s, g_s, vn_s,
                  gu_s, mix_s, sgu_s, t_s, mg_s, x1_s, hp_s):
    ts = x_ref.shape[1]
    n_chunks = ts // CHUNK
    dk, dv, gd = RET_QK_DIM, RET_V_DIM, SGU_GROUP_DIM
    half = D_MODEL // 2
    halves = (slice(0, half), slice(half, D_MODEL))
    row_blocks = [slice(c * CHUNK, (c + 1) * CHUNK) for c in range(n_chunks)]
    step = pl.program_id(0)

    @pl.when(step == 0)
    def _():
        mg_s[...] = jnp.zeros_like(mg_s)

    @pl.when(step % tiles_per_seq == 0)
    def _():
        state_ref[...] = jnp.zeros_like(state_ref)

    def proj(col0, hf):
        c0 = col0 + hf * half
        return _wdot(h_s[...], win_ref[:, c0:c0 + half])

    def rotary(val, cos, sin, row_factor_ref, plain_s, scaled_s):
        for hd in range(RET_HEADS):
            lanes = slice(hd * dk, (hd + 1) * dk)
            xh = val[:, lanes]
            xr = xh * cos + pltpu.roll(xh, dk // 2, 1) * sin
            plain_s[:, lanes] = xr.astype(_BF16)
            scaled_s[:, lanes] = (xr * row_factor_ref[hd]).astype(_BF16)

    def head_slices(c, hd):
        return row_blocks[c], slice(hd * dk, (hd + 1) * dk), slice(hd * dv, (hd + 1) * dv)

    def mix_unit(c, g):
        rows = row_blocks[c]
        gl = slice(g * gd, (g + 1) * gd)
        mix_s[rows, gl] = _dot(wsc_ref[g], vn_s[rows, gl]) + bs_ref[g]

    units = [(c, hd) for c in range(n_chunks) for hd in range(RET_HEADS)]


    for rows in row_blocks:
        x1 = xl_ref[0, rows, :] + _wdot(mg_s[rows, :], wo_ref[...])
        x1_s[rows, :] = x1
        hp_s[rows, :] = _rms(x1, gp_ref[...]).astype(_BF16)
    h_s[...] = _rms(x_ref[0], gm_ref[...]).astype(_BF16)
    q_raw = proj(_QK0, 0)
    k_raw = proj(_QK0, 1)
    pg = [_wdot(hp_s[rows, :], wpg_ref[...]) for rows in row_blocks]
    rotary(q_raw, cq_ref[...], sq_ref[...], xi_ref, q_s, qx_s)
    v_s[:, halves[0]] = proj(_V0, 0).astype(_BF16)
    rotary(k_raw, ck_ref[...], sk_ref[...], zeta_ref, k_s, kz_s)
    v_s[:, halves[1]] = proj(_V0, 1).astype(_BF16)
    pp = _wdot(pl_ref[0].astype(_BF16), wpp_ref[...])
    for c, rows in enumerate(row_blocks):
        x2 = x1_s[rows, :] + jax.nn.sigmoid(pg[c]) * pp[rows, :]
        o_ref[0, rows, :] = _rms(x2, gf_ref[...])

    sv0 = proj(_SV0, 0)
    sv1 = proj(_SV0, 1)
    g_s[:, halves[0]] = _gelu(sv0)
    rg0 = proj(_RG0, 0)

    scores, kv = {}, {}
    for c, hd in units:
        rows, lanes, vl = head_slices(c, hd)
        scores[c, hd] = lax.dot_general(q_s[rows, lanes], k_s[rows, lanes],
                                        (((1,), (1,)), ((), ())),
                                        preferred_element_type=_F32)
    g_s[:, halves[1]] = _gelu(sv1)
    for c, hd in units:
        rows, lanes, vl = head_slices(c, hd)
        kv[c, hd] = lax.dot_general(kz_s[rows, lanes], v_s[rows, vl],
                                    (((0,), (0,)), ((), ())),
                                    preferred_element_type=_F32)
    rg1 = proj(_RG0, 1)
    gate_s[:, halves[0]] = _silu(rg0)
    prev = {}
    for hd in range(RET_HEADS):
        st = state_ref[hd]
        for c in range(n_chunks):
            prev[c, hd] = st
            st = kv[c, hd] + cdec_ref[hd] * st
        state_ref[hd] = st
    r = {}
    for c, hd in units:
        rows, lanes, vl = head_slices(c, hd)
        lhs = jnp.concatenate([(scores[c, hd] * decay_ref[hd]).astype(_BF16),
                               qx_s[rows, lanes]], axis=1)
        rhs = jnp.concatenate([v_s[rows, vl], prev[c, hd].astype(_BF16)], axis=0)
        r[c, hd] = _dot(lhs, rhs)
    gate_s[:, halves[1]] = _silu(rg1)

    su0 = proj(_SU0, 0)
    vn_s[...] = _unit(g_s[...], GN_EPS).astype(_BF16)
    su1 = proj(_SU0, 1)
    for c, hd in units:
        rows, lanes, vl = head_slices(c, hd)
        ret_s[rows, vl] = (_unit(r[c, hd], GN_EPS) * gate_s[rows, vl]).astype(_BF16)
    sg0 = proj(_SG0, 0)
    for c in range(n_chunks):
        for g in range(SGU_GROUPS):
            mix_unit(c, g)
    gu_s[:, halves[0]] = _gelu(su0)
    sg1 = proj(_SG0, 1)
    ro0 = _wdot(ret_s[...], wro_ref[:, halves[0]])
    gu_s[:, halves[1]] = _gelu(su1)
    ro1 = _wdot(ret_s[...], wro_ref[:, halves[1]])
    sgu_s[:, halves[0]] = (gu_s[:, halves[0]] * mix_s[:, halves[0]] * _silu(sg0)).astype(_BF16)
    mr0 = proj(_MR0, 0)
    sgu_s[:, halves[1]] = (gu_s[:, halves[1]] * mix_s[:, halves[1]] * _silu(sg1)).astype(_BF16)
    mr1 = proj(_MR0, 1)
    ms0 = proj(_MS0, 0)
    t_s[:, halves[0]] = jax.nn.sigmoid(mr0) * ro0
    ms1 = proj(_MS0, 1)
    t_s[:, halves[1]] = jax.nn.sigmoid(mr1) * ro1
    so0 = _wdot(sgu_s[...], wso_ref[:, halves[0]])
    sm0 = jax.nn.sigmoid(ms0)
    so1 = _wdot(sgu_s[...], wso_ref[:, halves[1]])
    sm1 = jax.nn.sigmoid(ms1)
    mg_s[:, halves[0]] = (t_s[:, halves[0]] + sm0 * so0).astype(_BF16)
    mg_s[:, halves[1]] = (t_s[:, halves[1]] + sm1 * so1).astype(_BF16)


def _position_tables(seq, ts):
    dk = RET_QK_DIM
    half = dk // 2
    inv = ROPE_BASE ** (-jnp.arange(half, dtype=_F32) / half)
    ang = jnp.arange(seq, dtype=_F32)[:, None] * inv[None, :]
    cos, sin = jnp.cos(ang), jnp.sin(ang)
    cos_full = jnp.concatenate([cos, cos], axis=-1)
    sin_signed = jnp.concatenate([-sin, sin], axis=-1)
    scale = dk ** -0.5

    log_g = jnp.log(1.0 - 2.0 ** (-5.0 - jnp.arange(RET_HEADS, dtype=_F32)))
    idx = jnp.arange(CHUNK, dtype=_F32)
    diff = idx[:, None] - idx[None, :]
    decay = jnp.where(diff[None] >= 0,
                      jnp.exp(jnp.maximum(diff, 0.0)[None] * log_g[:, None, None]), 0.0)
    zeta = jnp.exp((CHUNK - 1.0 - idx)[None, :] * log_g[:, None])
    xi = jnp.exp((idx + 1.0)[None, :] * log_g[:, None])
    cdec = jnp.exp(CHUNK * log_g)
    reps = ts // CHUNK
    tile = lambda t: jnp.broadcast_to(jnp.tile(t, (1, reps))[:, :, None],
                                      (RET_HEADS, ts, dk))
    return (cos_full * scale, sin_signed * scale, cos_full, sin_signed,
            tile(xi), tile(zeta), decay,
            jnp.broadcast_to(cdec[:, None, None], (RET_HEADS, 1, RET_V_DIM)))


def _const_spec(shape):
    zeros = (0,) * len(shape)
    return pl.BlockSpec(shape, lambda i: zeros, pipeline_mode=pl.Buffered(1))


@jax.jit
def kernel(x, p, w_in, w_ret_out, w_sgu_out, w_out, sgu_ws, sgu_bs, w_ple_gate,
           w_ple_proj, g_mixer, g_ple, g_final):
    batch, seq, d = x.shape
    depth = w_in.shape[0]
    ts = TOKENS_PER_STEP
    assert depth == 1 and d == D_MODEL and seq % ts == 0 and ts % CHUNK == 0
    nt = seq // ts
    n_tiles = batch * nt
    cq, sq, ck, sk, xi, zeta, decay, cdec = _position_tables(seq, ts)
    tril = jnp.tril(jnp.ones((CHUNK, CHUNK), _F32))

    cur = lambda i: jnp.minimum(i, n_tiles - 1)
    lag = lambda i: jnp.maximum(i - 1, 0)
    pos_spec = pl.BlockSpec((ts, RET_QK_DIM), lambda i: (cur(i) % nt, 0))
    cur_spec = lambda width: pl.BlockSpec(
        (1, ts, width), lambda i: (cur(i) // nt, cur(i) % nt, 0))
    lag_spec = lambda width: pl.BlockSpec(
        (1, ts, width), lambda i: (lag(i) // nt, lag(i) % nt, 0))

    ws_causal = (sgu_ws[0] * tril[None]).astype(_BF16)
    bs_rows = jnp.broadcast_to(sgu_bs[0][:, :, None],
                               (SGU_GROUPS, CHUNK, SGU_GROUP_DIM)).astype(_F32)
    consts = (xi, zeta, decay, cdec,
              _pack_rows(w_in[0]), _pack_rows(w_ret_out[0]),
              _pack_rows(w_sgu_out[0]), _pack_rows(w_out[0]),
              ws_causal, bs_rows,
              _pack_rows(w_ple_gate[0]), _pack_rows(w_ple_proj[0]),
              g_mixer[0][None, :], g_ple[0][None, :], g_final[None, :])
    act = lambda width, dt: pltpu.VMEM((ts, width), dt)
    return pl.pallas_call(
        functools.partial(_block_kernel, nt),
        grid=(n_tiles + 1,),
        in_specs=[cur_spec(D_MODEL), lag_spec(D_MODEL), lag_spec(PLE_DIM),
                  pos_spec, pos_spec, pos_spec, pos_spec]
                 + [_const_spec(c.shape) for c in consts],
        out_specs=lag_spec(D_MODEL),
        out_shape=jax.ShapeDtypeStruct(x.shape, x.dtype),
        scratch_shapes=[
            pltpu.VMEM((RET_HEADS, RET_QK_DIM, RET_V_DIM), _F32),
            act(D_MODEL, _BF16),
            act(RET_HEADS * RET_QK_DIM, _BF16),
            act(RET_HEADS * RET_QK_DIM, _BF16),
            act(RET_HEADS * RET_QK_DIM, _BF16),
            act(RET_HEADS * RET_QK_DIM, _BF16),
            act(D_MODEL, _BF16),
            act(D_MODEL, _F32),
            act(D_MODEL, _BF16),
            act(D_MODEL, _F32),
            act(D_MODEL, _BF16),
            act(D_MODEL, _F32),
            act(D_MODEL, _F32),
            act(D_MODEL, _BF16),
            act(D_MODEL, _F32),
            act(D_MODEL, _BF16),
            act(D_MODEL, _F32),
            act(D_MODEL, _BF16),
        ],
        compiler_params=pltpu.CompilerParams(
            dimension_semantics=("arbitrary",),
            vmem_limit_bytes=V7X_VMEM_LIMIT_BYTES),
        name="hybrid_block",
    )(x, x, p[0], cq, sq, ck, sk, *consts)
```

```python
import functools

import jax
import jax.numpy as jnp
from jax import lax
from jax.experimental import pallas as pl
from jax.experimental.pallas import tpu as pltpu

D_MODEL = 1024
PLE_DIM = 256
CHUNK = 128
RET_HEADS = 4
RET_V_DIM = D_MODEL // RET_HEADS
RET_QK_DIM = RET_V_DIM // 2
SGU_GROUPS = 4
SGU_GROUP_DIM = D_MODEL // SGU_GROUPS
ROPE_BASE = 10000.0
NORM_EPS = 1e-6
GN_EPS = 1e-5

_QK0 = 0
_V0 = 2 * RET_HEADS * RET_QK_DIM
_RG0 = _V0 + D_MODEL
_SU0 = _RG0 + D_MODEL
_SV0 = _SU0 + D_MODEL
_SG0 = _SV0 + D_MODEL
_MR0 = _SG0 + D_MODEL
_MS0 = _MR0 + D_MODEL
IN_WIDTH = _MS0 + D_MODEL

TOKENS_PER_STEP = 256
V7X_VMEM_LIMIT_BYTES = 56 * 1024 * 1024

_F32 = jnp.float32
_BF16 = jnp.bfloat16


def _dot(a, b):
    return jnp.dot(a, b, preferred_element_type=_F32)


def _pack_rows(w):
    k, n = w.shape
    bits = lax.bitcast_convert_type(w.astype(_BF16), jnp.uint16).astype(jnp.uint32)
    bits = bits.reshape(k // 2, 2, n)
    return bits[:, 0, :] | (bits[:, 1, :] << 16)


def _wdot(a, w_packed):
    return _dot(a, pltpu.bitcast(w_packed, _BF16))


def _rms(x, g):
    ms = jnp.mean(x * x, axis=-1, keepdims=True)
    return x * lax.rsqrt(ms + NORM_EPS) * g


def _unit(x, eps):
    mu = jnp.mean(x, axis=-1, keepdims=True)
    d = x - mu
    var = jnp.mean(d * d, axis=-1, keepdims=True)
    return d * lax.rsqrt(var + eps)


def _silu(x):
    return x * jax.nn.sigmoid(x)


def _gelu(x):
    return 0.5 * x * (1.0 + lax.erf(x * (0.5 ** 0.5)))


def _block_kernel(tiles_per_seq,
                  x_ref, xl_ref, pl_ref, cq_ref, sq_ref, ck_ref, sk_ref, xi_ref, zeta_ref,
                  decay_ref, cdec_ref, win_ref, wro_ref, wso_ref, wo_ref, wsc_ref,
                  bs_ref, wpg_ref, wpp_ref, gm_ref, gp_ref, gf_ref,
                  o_ref,
                  state_ref, h_s, q_s, qx_s, k_s, kz_s, v_s, gate_s, ret_s, g_s, vn_s,
                  gu_s, mix_s, sgu_s, t_s, mg_s, x1_s, hp_s):
    ts = x_ref.shape[1]
    n_chunks = ts // CHUNK
    dk, dv, gd = RET_QK_DIM, RET_V_DIM, SGU_GROUP_DIM
    half = D_MODEL // 2
    halves = (slice(0, half), slice(half, D_MODEL))
    row_blocks = [slice(c * CHUNK, (c + 1) * CHUNK) for c in range(n_chunks)]
    step = pl.program_id(0)

    @pl.when(step == 0)
    def _():
        mg_s[...] = jnp.zeros_like(mg_s)

    @pl.when(step % tiles_per_seq == 0)
    def _():
        state_ref[...] = jnp.zeros_like(state_ref)

    def proj(col0, hf):
        c0 = col0 + hf * half
        return _wdot(h_s[...], win_ref[:, c0:c0 + half])

    def rotary(val, cos, sin, row_factor_ref, plain_s, scaled_s):
        for hd in range(RET_HEADS):
            lanes = slice(hd * dk, (hd + 1) * dk)
            xh = val[:, lanes]
            xr = xh * cos + pltpu.roll(xh, dk // 2, 1) * sin
            plain_s[:, lanes] = xr.astype(_BF16)
            scaled_s[:, lanes] = (xr * row_factor_ref[hd]).astype(_BF16)

    def head_slices(c, hd):
        return row_blocks[c], slice(hd * dk, (hd + 1) * dk), slice(hd * dv, (hd + 1) * dv)

    def mix_unit(c, g):
        rows = row_blocks[c]
        gl = slice(g * gd, (g + 1) * gd)
        mix_s[rows, gl] = _dot(wsc_ref[g], vn_s[rows, gl]) + bs_ref[g]

    units = [(c, hd) for c in range(n_chunks) for hd in range(RET_HEADS)]


    for rows in row_blocks:
        x1 = xl_ref[0, rows, :] + _wdot(mg_s[rows, :], wo_ref[...])
        x1_s[rows, :] = x1
        hp_s[rows, :] = _rms(x1, gp_ref[...]).astype(_BF16)
    h_s[...] = _rms(x_ref[0], gm_ref[...]).astype(_BF16)
    q_raw = proj(_QK0, 0)
    k_raw = proj(_QK0, 1)
    pg = [_wdot(hp_s[rows, :], wpg_ref[...]) for rows in row_blocks]
    rotary(q_raw, cq_ref[...], sq_ref[...], xi_ref, q_s, qx_s)
    sv0 = proj(_SV0, 0)
    rotary(k_raw, ck_ref[...], sk_ref[...], zeta_ref, k_s, kz_s)
    v_s[:, halves[0]] = proj(_V0, 0).astype(_BF16)
    g_s[:, halves[0]] = _gelu(sv0)
    sv1 = proj(_SV0, 1)
    v_s[:, halves[1]] = proj(_V0, 1).astype(_BF16)
    pp = _wdot(pl_ref[0].astype(_BF16), wpp_ref[...])
    for c, rows in enumerate(row_blocks):
        x2 = x1_s[rows, :] + jax.nn.sigmoid(pg[c]) * pp[rows, :]
        o_ref[0, rows, :] = _rms(x2, gf_ref[...])
    rg0 = proj(_RG0, 0)

    scores, kv = {}, {}
    for c, hd in units:
        rows, lanes, vl = head_slices(c, hd)
        scores[c, hd] = lax.dot_general(q_s[rows, lanes], k_s[rows, lanes],
                                        (((1,), (1,)), ((), ())),
                                        preferred_element_type=_F32)
    g_s[:, halves[1]] = _gelu(sv1)
    for c, hd in units:
        rows, lanes, vl = head_slices(c, hd)
        kv[c, hd] = lax.dot_general(kz_s[rows, lanes], v_s[rows, vl],
                                    (((0,), (0,)), ((), ())),
                                    preferred_element_type=_F32)
    rg1 = proj(_RG0, 1)
    gate_s[:, halves[0]] = _silu(rg0)
    prev = {}
    for hd in range(RET_HEADS):
        st = state_ref[hd]
        for c in range(n_chunks):
            prev[c, hd] = st
            st = kv[c, hd] + cdec_ref[hd] * st
        state_ref[hd] = st
    r = {}
    for c, hd in units:
        rows, lanes, vl = head_slices(c, hd)
        lhs = jnp.concatenate([(scores[c, hd] * decay_ref[hd]).astype(_BF16),
                               qx_s[rows, lanes]], axis=1)
        rhs = jnp.concatenate([v_s[rows, vl], prev[c, hd].astype(_BF16)], axis=0)
        r[c, hd] = _dot(lhs, rhs)
    gate_s[:, halves[1]] = _silu(rg1)

    su0 = proj(_SU0, 0)
    vn_s[...] = _unit(g_s[...], GN_EPS).astype(_BF16)
    su1 = proj(_SU0, 1)
    for c, hd in units:
        rows, lanes, vl = head_slices(c, hd)
        ret_s[rows, vl] = (_unit(r[c, hd], GN_EPS) * gate_s[rows, vl]).astype(_BF16)
    sg0 = proj(_SG0, 0)
    for c in range(n_chunks):
        for g in range(SGU_GROUPS):
            mix_unit(c, g)
    gu_s[:, halves[0]] = _gelu(su0)
    sg1 = proj(_SG0, 1)
    ro0 = _wdot(ret_s[...], wro_ref[:, halves[0]])
    gu_s[:, halves[1]] = _gelu(su1)
    ro1 = _wdot(ret_s[...], wro_ref[:, halves[1]])
    sgu_s[:, halves[0]] = (gu_s[:, halves[0]] * mix_s[:, halves[0]] * _silu(sg0)).astype(_BF16)
    mr0 = proj(_MR0, 0)
    sgu_s[:, halves[1]] = (gu_s[:, halves[1]] * mix_s[:, halves[1]] * _silu(sg1)).astype(_BF16)
    mr1 = proj(_MR0, 1)
    ms0 = proj(_MS0, 0)
    t_s[:, halves[0]] = jax.nn.sigmoid(mr0) * ro0
    ms1 = proj(_MS0, 1)
    t_s[:, halves[1]] = jax.nn.sigmoid(mr1) * ro1
    so0 = _wdot(sgu_s[...], wso_ref[:, halves[0]])
    sm0 = jax.nn.sigmoid(ms0)
    so1 = _wdot(sgu_s[...], wso_ref[:, halves[1]])
    sm1 = jax.nn.sigmoid(ms1)
    mg_s[:, halves[0]] = (t_s[:, halves[0]] + sm0 * so0).astype(_BF16)
    mg_s[:, halves[1]] = (t_s[:, halves[1]] + sm1 * so1).astype(_BF16)


def _position_tables(seq, ts):
    dk = RET_QK_DIM
    half = dk // 2
    inv = ROPE_BASE ** (-jnp.arange(half, dtype=_F32) / half)
    ang = jnp.arange(seq, dtype=_F32)[:, None] * inv[None, :]
    cos, sin = jnp.cos(ang), jnp.sin(ang)
    cos_full = jnp.concatenate([cos, cos], axis=-1)
    sin_signed = jnp.concatenate([-sin, sin], axis=-1)
    scale = dk ** -0.5

    log_g = jnp.log(1.0 - 2.0 ** (-5.0 - jnp.arange(RET_HEADS, dtype=_F32)))
    idx = jnp.arange(CHUNK, dtype=_F32)
    diff = idx[:, None] - idx[None, :]
    decay = jnp.where(diff[None] >= 0,
                      jnp.exp(jnp.maximum(diff, 0.0)[None] * log_g[:, None, None]), 0.0)
    zeta = jnp.exp((CHUNK - 1.0 - idx)[None, :] * log_g[:, None])
    xi = jnp.exp((idx + 1.0)[None, :] * log_g[:, None])
    cdec = jnp.exp(CHUNK * log_g)
    reps = ts // CHUNK
    tile = lambda t: jnp.broadcast_to(jnp.tile(t, (1, reps))[:, :, None],
                                      (RET_HEADS, ts, dk))
    return (cos_full * scale, sin_signed * scale, cos_full, sin_signed,
            tile(xi), tile(zeta), decay,
            jnp.broadcast_to(cdec[:, None, None], (RET_HEADS, 1, RET_V_DIM)))


def _const_spec(shape):
    zeros = (0,) * len(shape)
    return pl.BlockSpec(shape, lambda i: zeros, pipeline_mode=pl.Buffered(1))


@jax.jit
def kernel(x, p, w_in, w_ret_out, w_sgu_out, w_out, sgu_ws, sgu_bs, w_ple_gate,
           w_ple_proj, g_mixer, g_ple, g_final):
    batch, seq, d = x.shape
    depth = w_in.shape[0]
    ts = TOKENS_PER_STEP
    assert depth == 1 and d == D_MODEL and seq % ts == 0 and ts % CHUNK == 0
    nt = seq // ts
    n_tiles = batch * nt
    cq, sq, ck, sk, xi, zeta, decay, cdec = _position_tables(seq, ts)
    tril = jnp.tril(jnp.ones((CHUNK, CHUNK), _F32))

    cur = lambda i: jnp.minimum(i, n_tiles - 1)
    lag = lambda i: jnp.maximum(i - 1, 0)
    pos_spec = pl.BlockSpec((ts, RET_QK_DIM), lambda i: (cur(i) % nt, 0))
    cur_spec = lambda width: pl.BlockSpec(
        (1, ts, width), lambda i: (cur(i) // nt, cur(i) % nt, 0))
    lag_spec = lambda width: pl.BlockSpec(
        (1, ts, width), lambda i: (lag(i) // nt, lag(i) % nt, 0))

    ws_causal = (sgu_ws[0] * tril[None]).astype(_BF16)
    bs_rows = jnp.broadcast_to(sgu_bs[0][:, :, None],
                               (SGU_GROUPS, CHUNK, SGU_GROUP_DIM)).astype(_F32)
    consts = (xi, zeta, decay, cdec,
              _pack_rows(w_in[0]), _pack_rows(w_ret_out[0]),
              _pack_rows(w_sgu_out[0]), _pack_rows(w_out[0]),
              ws_causal, bs_rows,
              _pack_rows(w_ple_gate[0]), _pack_rows(w_ple_proj[0]),
              g_mixer[0][None, :], g_ple[0][None, :], g_final[None, :])
    act = lambda width, dt: pltpu.VMEM((ts, width), dt)
    return pl.pallas_call(
        functools.partial(_block_kernel, nt),
        grid=(n_tiles + 1,),
        in_specs=[cur_spec(D_MODEL), lag_spec(D_MODEL), lag_spec(PLE_DIM),
                  pos_spec, pos_spec, pos_spec, pos_spec]
                 + [_const_spec(c.shape) for c in consts],
        out_specs=lag_spec(D_MODEL),
        out_shape=jax.ShapeDtypeStruct(x.shape, x.dtype),
        scratch_shapes=[
            pltpu.VMEM((RET_HEADS, RET_QK_DIM, RET_V_DIM), _F32),
            act(D_MODEL, _BF16),
            act(RET_HEADS * RET_QK_DIM, _BF16),
            act(RET_HEADS * RET_QK_DIM, _BF16),
            act(RET_HEADS * RET_QK_DIM, _BF16),
            act(RET_HEADS * RET_QK_DIM, _BF16),
            act(D_MODEL, _BF16),
            act(D_MODEL, _F32),
            act(D_MODEL, _BF16),
            act(D_MODEL, _F32),
            act(D_MODEL, _BF16),
            act(D_MODEL, _F32),
            act(D_MODEL, _F32),
            act(D_MODEL, _BF16),
            act(D_MODEL, _F32),
            act(D_MODEL, _BF16),
            act(D_MODEL, _F32),
            act(D_MODEL, _BF16),
        ],
        compiler_params=pltpu.CompilerParams(
            dimension_semantics=("arbitrary",),
            vmem_limit_bytes=V7X_VMEM_LIMIT_BYTES),
        name="hybrid_block",
    )(x, x, p[0], cq, sq, ck, sk, *consts)
```

```python
import functools

import jax
import jax.numpy as jnp
from jax import lax
from jax.experimental import pallas as pl
from jax.experimental.pallas import tpu as pltpu

D_MODEL = 1024
PLE_DIM = 256
CHUNK = 128
RET_HEADS = 4
RET_V_DIM = D_MODEL // RET_HEADS
RET_QK_DIM = RET_V_DIM // 2
SGU_GROUPS = 4
SGU_GROUP_DIM = D_MODEL // SGU_GROUPS
ROPE_BASE = 10000.0
NORM_EPS = 1e-6
GN_EPS = 1e-5

_QK0 = 0
_V0 = 2 * RET_HEADS * RET_QK_DIM
_RG0 = _V0 + D_MODEL
_SU0 = _RG0 + D_MODEL
_SV0 = _SU0 + D_MODEL
_SG0 = _SV0 + D_MODEL
_MR0 = _SG0 + D_MODEL
_MS0 = _MR0 + D_MODEL
IN_WIDTH = _MS0 + D_MODEL

TOKENS_PER_STEP = 256
V7X_VMEM_LIMIT_BYTES = 56 * 1024 * 1024
PACK_BLOCK_ROWS = 512
PACK_BLOCK_COLS = 2048

_F32 = jnp.float32
_BF16 = jnp.bfloat16


def _dot(a, b):
    return jnp.dot(a, b, preferred_element_type=_F32)


def _pack_kernel(w_ref, o_ref):
    o_ref[...] = pltpu.bitcast(w_ref[...].astype(_BF16), jnp.uint32)


def _pack_rows(w):
    k, n = w.shape
    kb, nb = min(k, PACK_BLOCK_ROWS), min(n, PACK_BLOCK_COLS)
    assert k % kb == 0 and n % nb == 0
    return pl.pallas_call(
        _pack_kernel,
        grid=(k // kb, n // nb),
        in_specs=[pl.BlockSpec((kb, nb), lambda i, j: (i, j))],
        out_specs=pl.BlockSpec((kb // 2, nb), lambda i, j: (i, j)),
        out_shape=jax.ShapeDtypeStruct((k // 2, n), jnp.uint32),
        name="pack_weight_rows",
    )(w)


def _wdot(a, w_packed):
    return _dot(a, pltpu.bitcast(w_packed, _BF16))


def _rms(x, g):
    ms = jnp.mean(x * x, axis=-1, keepdims=True)
    return x * lax.rsqrt(ms + NORM_EPS) * g


def _unit(x, eps):
    mu = jnp.mean(x, axis=-1, keepdims=True)
    d = x - mu
    var = jnp.mean(d * d, axis=-1, keepdims=True)
    return d * lax.rsqrt(var + eps)


def _silu(x):
    return x * jax.nn.sigmoid(x)


def _gelu(x):
    return 0.5 * x * (1.0 + lax.erf(x * (0.5 ** 0.5)))


def _block_kernel(tiles_per_seq,
                  x_ref, xl_ref, pl_ref, cq_ref, sq_ref, ck_ref, sk_ref, xi_ref, zeta_ref,
                  decay_ref, cdec_ref, win_ref, wro_ref, wso_ref, wo_ref, wsc_ref,
                  bs_ref, wpg_ref, wpp_ref, gm_ref, gp_ref, gf_ref,
                  o_ref,
                  state_ref, h_s, q_s, qx_s, k_s, kz_s, v_s, gate_s, ret_s, g_s, vn_s,
                  gu_s, mix_s, sgu_s, t_s, mg_s, x1_s, hp_s):
    ts = x_ref.shape[1]
    n_chunks = ts // CHUNK
    dk, dv, gd = RET_QK_DIM, RET_V_DIM, SGU_GROUP_DIM
    half = D_MODEL // 2
    halves = (slice(0, half), slice(half, D_MODEL))
    row_blocks = [slice(c * CHUNK, (c + 1) * CHUNK) for c in range(n_chunks)]
    step = pl.program_id(0)

    @pl.when(step == 0)
    def _():
        mg_s[...] = jnp.zeros_like(mg_s)

    @pl.when(step % tiles_per_seq == 0)
    def _():
        state_ref[...] = jnp.zeros_like(state_ref)

    def proj(col0, hf):
        c0 = col0 + hf * half
        return _wdot(h_s[...], win_ref[:, c0:c0 + half])

    def rotary(val, cos, sin, row_factor_ref, plain_s, scaled_s):
        for hd in range(RET_HEADS):
            lanes = slice(hd * dk, (hd + 1) * dk)
            xh = val[:, lanes]
            xr = xh * cos + pltpu.roll(xh, dk // 2, 1) * sin
            plain_s[:, lanes] = xr.astype(_BF16)
            scaled_s[:, lanes] = (xr * row_factor_ref[hd]).astype(_BF16)

    def head_slices(c, hd):
        return row_blocks[c], slice(hd * dk, (hd + 1) * dk), slice(hd * dv, (hd + 1) * dv)

    def mix_unit(c, g):
        rows = row_blocks[c]
        gl = slice(g * gd, (g + 1) * gd)
        mix_s[rows, gl] = _dot(wsc_ref[g], vn_s[rows, gl]) + bs_ref[g]

    units = [(c, hd) for c in range(n_chunks) for hd in range(RET_HEADS)]


    for rows in row_blocks:
        x1 = xl_ref[0, rows, :] + _wdot(mg_s[rows, :], wo_ref[...])
        x1_s[rows, :] = x1
        hp_s[rows, :] = _rms(x1, gp_ref[...]).astype(_BF16)
    h_s[...] = _rms(x_ref[0], gm_ref[...]).astype(_BF16)
    q_raw = proj(_QK0, 0)
    k_raw = proj(_QK0, 1)
    pg = [_wdot(hp_s[rows, :], wpg_ref[...]) for rows in row_blocks]
    rotary(q_raw, cq_ref[...], sq_ref[...], xi_ref, q_s, qx_s)
    sv0 = proj(_SV0, 0)
    rotary(k_raw, ck_ref[...], sk_ref[...], zeta_ref, k_s, kz_s)
    v_s[:, halves[0]] = proj(_V0, 0).astype(_BF16)
    g_s[:, halves[0]] = _gelu(sv0)
    sv1 = proj(_SV0, 1)
    v_s[:, halves[1]] = proj(_V0, 1).astype(_BF16)
    pp = _wdot(pl_ref[0].astype(_BF16), wpp_ref[...])
    for c, rows in enumerate(row_blocks):
        x2 = x1_s[rows, :] + jax.nn.sigmoid(pg[c]) * pp[rows, :]
        o_ref[0, rows, :] = _rms(x2, gf_ref[...])
    rg0 = proj(_RG0, 0)

    scores, kv = {}, {}
    for c, hd in units:
        rows, lanes, vl = head_slices(c, hd)
        scores[c, hd] = lax.dot_general(q_s[rows, lanes], k_s[rows, lanes],
                                        (((1,), (1,)), ((), ())),
                                        preferred_element_type=_F32)
    g_s[:, halves[1]] = _gelu(sv1)
    for c, hd in units:
        rows, lanes, vl = head_slices(c, hd)
        kv[c, hd] = lax.dot_general(kz_s[rows, lanes], v_s[rows, vl],
                                    (((0,), (0,)), ((), ())),
                                    preferred_element_type=_F32)
    rg1 = proj(_RG0, 1)
    gate_s[:, halves[0]] = _silu(rg0)
    prev = {}
    for hd in range(RET_HEADS):
        st = state_ref[hd]
        for c in range(n_chunks):
            prev[c, hd] = st
            st = kv[c, hd] + cdec_ref[hd] * st
        state_ref[hd] = st
    r = {}
    for c, hd in units:
        rows, lanes, vl = head_slices(c, hd)
        lhs = jnp.concatenate([(scores[c, hd] * decay_ref[hd]).astype(_BF16),
                               qx_s[rows, lanes]], axis=1)
        rhs = jnp.concatenate([v_s[rows, vl], prev[c, hd].astype(_BF16)], axis=0)
        r[c, hd] = _dot(lhs, rhs)
    gate_s[:, halves[1]] = _silu(rg1)

    su0 = proj(_SU0, 0)
    vn_s[...] = _unit(g_s[...], GN_EPS).astype(_BF16)
    su1 = proj(_SU0, 1)
    for c, hd in units:
        rows, lanes, vl = head_slices(c, hd)
        ret_s[rows, vl] = (_unit(r[c, hd], GN_EPS) * gate_s[rows, vl]).astype(_BF16)
    sg0 = proj(_SG0, 0)
    for c in range(n_chunks):
        for g in range(SGU_GROUPS):
            mix_unit(c, g)
    gu_s[:, halves[0]] = _gelu(su0)
    sg1 = proj(_SG0, 1)
    ro0 = _wdot(ret_s[...], wro_ref[:, halves[0]])
    gu_s[:, halves[1]] = _gelu(su1)
    ro1 = _wdot(ret_s[...], wro_ref[:, halves[1]])
    sgu_s[:, halves[0]] = (gu_s[:, halves[0]] * mix_s[:, halves[0]] * _silu(sg0)).astype(_BF16)
    mr0 = proj(_MR0, 0)
    sgu_s[:, halves[1]] = (gu_s[:, halves[1]] * mix_s[:, halves[1]] * _silu(sg1)).astype(_BF16)
    mr1 = proj(_MR0, 1)
    ms0 = proj(_MS0, 0)
    t_s[:, halves[0]] = jax.nn.sigmoid(mr0) * ro0
    ms1 = proj(_MS0, 1)
    t_s[:, halves[1]] = jax.nn.sigmoid(mr1) * ro1
    so0 = _wdot(sgu_s[...], wso_ref[:, halves[0]])
    sm0 = jax.nn.sigmoid(ms0)
    so1 = _wdot(sgu_s[...], wso_ref[:, halves[1]])
    sm1 = jax.nn.sigmoid(ms1)
    mg_s[:, halves[0]] = (t_s[:, halves[0]] + sm0 * so0).astype(_BF16)
    mg_s[:, halves[1]] = (t_s[:, halves[1]] + sm1 * so1).astype(_BF16)


def _position_tables(seq, ts):
    dk = RET_QK_DIM
    half = dk // 2
    inv = ROPE_BASE ** (-jnp.arange(half, dtype=_F32) / half)
    ang = jnp.arange(seq, dtype=_F32)[:, None] * inv[None, :]
    cos, sin = jnp.cos(ang), jnp.sin(ang)
    cos_full = jnp.concatenate([cos, cos], axis=-1)
    sin_signed = jnp.concatenate([-sin, sin], axis=-1)
    scale = dk ** -0.5

    log_g = jnp.log(1.0 - 2.0 ** (-5.0 - jnp.arange(RET_HEADS, dtype=_F32)))
    idx = jnp.arange(CHUNK, dtype=_F32)
    diff = idx[:, None] - idx[None, :]
    decay = jnp.where(diff[None] >= 0,
                      jnp.exp(jnp.maximum(diff, 0.0)[None] * log_g[:, None, None]), 0.0)
    zeta = jnp.exp((CHUNK - 1.0 - idx)[None, :] * log_g[:, None])
    xi = jnp.exp((idx + 1.0)[None, :] * log_g[:, None])
    cdec = jnp.exp(CHUNK * log_g)
    reps = ts // CHUNK
    tile = lambda t: jnp.broadcast_to(jnp.tile(t, (1, reps))[:, :, None],
                                      (RET_HEADS, ts, dk))
    return (cos_full * scale, sin_signed * scale, cos_full, sin_signed,
            tile(xi), tile(zeta), decay,
            jnp.broadcast_to(cdec[:, None, None], (RET_HEADS, 1, RET_V_DIM)))


def _const_spec(shape):
    zeros = (0,) * len(shape)
    return pl.BlockSpec(shape, lambda i: zeros, pipeline_mode=pl.Buffered(1))


@jax.jit
def kernel(x, p, w_in, w_ret_out, w_sgu_out, w_out, sgu_ws, sgu_bs, w_ple_gate,
           w_ple_proj, g_mixer, g_ple, g_final):
    batch, seq, d = x.shape
    depth = w_in.shape[0]
    ts = TOKENS_PER_STEP
    assert depth == 1 and d == D_MODEL and seq % ts == 0 and ts % CHUNK == 0
    nt = seq // ts
    n_tiles = batch * nt
    cq, sq, ck, sk, xi, zeta, decay, cdec = _position_tables(seq, ts)
    tril = jnp.tril(jnp.ones((CHUNK, CHUNK), _F32))

    cur = lambda i: jnp.minimum(i, n_tiles - 1)
    lag = lambda i: jnp.maximum(i - 1, 0)
    pos_spec = pl.BlockSpec((ts, RET_QK_DIM), lambda i: (cur(i) % nt, 0))
    cur_spec = lambda width: pl.BlockSpec(
        (1, ts, width), lambda i: (cur(i) // nt, cur(i) % nt, 0))
    lag_spec = lambda width: pl.BlockSpec(
        (1, ts, width), lambda i: (lag(i) // nt, lag(i) % nt, 0))

    ws_causal = (sgu_ws[0] * tril[None]).astype(_BF16)
    bs_rows = jnp.broadcast_to(sgu_bs[0][:, :, None],
                               (SGU_GROUPS, CHUNK, SGU_GROUP_DIM)).astype(_F32)
    consts = (xi, zeta, decay, cdec,
              _pack_rows(w_in[0]), _pack_rows(w_ret_out[0]),
              _pack_rows(w_sgu_out[0]), _pack_rows(w_out[0]),
              ws_causal, bs_rows,
              _pack_rows(w_ple_gate[0]), _pack_rows(w_ple_proj[0]),
              g_mixer[0][None, :], g_ple[0][None, :], g_final[None, :])
    act = lambda width, dt: pltpu.VMEM((ts, width), dt)
    return pl.pallas_call(
        functools.partial(_block_kernel, nt),
        grid=(n_tiles + 1,),
        in_specs=[cur_spec(D_MODEL), lag_spec(D_MODEL), lag_spec(PLE_DIM),
                  pos_spec, pos_spec, pos_spec, pos_spec]
                 + [_const_spec(c.shape) for c in consts],
        out_specs=lag_spec(D_MODEL),
        out_shape=jax.ShapeDtypeStruct(x.shape, x.dtype),
        scratch_shapes=[
            pltpu.VMEM((RET_HEADS, RET_QK_DIM, RET_V_DIM), _F32),
            act(D_MODEL, _BF16),
            act(RET_HEADS * RET_QK_DIM, _BF16),
            act(RET_HEADS * RET_QK_DIM, _BF16),
            act(RET_HEADS * RET_QK_DIM, _BF16),
            act(RET_HEADS * RET_QK_DIM, _BF16),
            act(D_MODEL, _BF16),
            act(D_MODEL, _F32),
            act(D_MODEL, _BF16),
            act(D_MODEL, _F32),
            act(D_MODEL, _BF16),
            act(D_MODEL, _F32),
            act(D_MODEL, _F32),
            act(D_MODEL, _BF16),
            act(D_MODEL, _F32),
            act(D_MODEL, _BF16),
            act(D_MODEL, _F32),
            act(D_MODEL, _BF16),
        ],
        compiler_params=pltpu.CompilerParams(
            dimension_semantics=("arbitrary",),
            vmem_limit_bytes=V7X_VMEM_LIMIT_BYTES),
        name="hybrid_block",
    )(x, x, p[0], cq, sq, ck, sk, *consts)
```

```python
import functools

import jax
import jax.numpy as jnp
from jax import lax
from jax.experimental import pallas as pl
from jax.experimental.pallas import tpu as pltpu

D_MODEL = 1024
PLE_DIM = 256
CHUNK = 128
RET_HEADS = 4
RET_V_DIM = D_MODEL // RET_HEADS
RET_QK_DIM = RET_V_DIM // 2
SGU_GROUPS = 4
SGU_GROUP_DIM = D_MODEL // SGU_GROUPS
ROPE_BASE = 10000.0
NORM_EPS = 1e-6
GN_EPS = 1e-5

_QK0 = 0
_V0 = 2 * RET_HEADS * RET_QK_DIM
_RG0 = _V0 + D_MODEL
_SU0 = _RG0 + D_MODEL
_SV0 = _SU0 + D_MODEL
_SG0 = _SV0 + D_MODEL
_MR0 = _SG0 + D_MODEL
_MS0 = _MR0 + D_MODEL
IN_WIDTH = _MS0 + D_MODEL

TOKENS_PER_STEP = 256
V7X_VMEM_LIMIT_BYTES = 56 * 1024 * 1024
PACK_BLOCK_ROWS = 512
PACK_BLOCK_COLS = 2048

_F32 = jnp.float32
_BF16 = jnp.bfloat16


def _dot(a, b):
    return jnp.dot(a, b, preferred_element_type=_F32)


def _pack_kernel(w_ref, o_ref):
    o_ref[...] = pltpu.bitcast(w_ref[...].astype(_BF16), jnp.uint32)


def _pack_rows(w):
    k, n = w.shape
    kb, nb = min(k, PACK_BLOCK_ROWS), min(n, PACK_BLOCK_COLS)
    assert k % kb == 0 and n % nb == 0
    return pl.pallas_call(
        _pack_kernel,
        grid=(k // kb, n // nb),
        in_specs=[pl.BlockSpec((kb, nb), lambda i, j: (i, j))],
        out_specs=pl.BlockSpec((kb // 2, nb), lambda i, j: (i, j)),
        out_shape=jax.ShapeDtypeStruct((k // 2, n), jnp.uint32),
        name="pack_weight_rows",
    )(w)


def _wdot(a, w_packed):
    return _dot(a, pltpu.bitcast(w_packed, _BF16))


def _rms(x, g):
    ms = jnp.mean(x * x, axis=-1, keepdims=True)
    return x * lax.rsqrt(ms + NORM_EPS) * g


def _unit(x, eps):
    mu = jnp.mean(x, axis=-1, keepdims=True)
    d = x - mu
    var = jnp.mean(d * d, axis=-1, keepdims=True)
    return d * lax.rsqrt(var + eps)


def _silu(x):
    return x * jax.nn.sigmoid(x)


def _gelu(x):
    return 0.5 * x * (1.0 + lax.erf(x * (0.5 ** 0.5)))


def _block_kernel(tiles_per_seq, n_steps,
                  x_ref, xl_ref, pl_ref, rope_ref, xi_ref, zeta_ref,
                  decay_ref, cdec_ref, win_ref, wro_ref, wso_ref, wo_ref, wsc_ref,
                  bs_ref, wpg_ref, wpp_ref, gm_ref, gp_ref, gf_ref,
                  o_ref,
                  state_ref, h_s, q_s, qx_s, k_s, kz_s, v_s, gate_s, ret_s, g_s, vn_s,
                  gu_s, mix_s, sgu_s, t_s, mg_s, x1_s, hp_s):
    ts = x_ref.shape[1]
    n_chunks = ts // CHUNK
    dk, dv, gd = RET_QK_DIM, RET_V_DIM, SGU_GROUP_DIM
    half = D_MODEL // 2
    halves = (slice(0, half), slice(half, D_MODEL))
    row_blocks = [slice(c * CHUNK, (c + 1) * CHUNK) for c in range(n_chunks)]
    step = pl.program_id(0)
    tile_in_seq = jnp.minimum(step, n_steps - 2) % tiles_per_seq
    pos_rows = pl.ds(pl.multiple_of(tile_in_seq * ts, ts), ts)
    rope = lambda j: rope_ref[pos_rows, j * dk:(j + 1) * dk]

    @pl.when(step == 0)
    def _():
        mg_s[...] = jnp.zeros_like(mg_s)

    @pl.when(step % tiles_per_seq == 0)
    def _():
        state_ref[...] = jnp.zeros_like(state_ref)

    def proj(col0, hf):
        c0 = col0 + hf * half
        return _wdot(h_s[...], win_ref[:, c0:c0 + half])

    def rotary(val, cos, sin, row_factor_ref, plain_s, scaled_s):
        for hd in range(RET_HEADS):
            lanes = slice(hd * dk, (hd + 1) * dk)
            xh = val[:, lanes]
            xr = xh * cos + pltpu.roll(xh, dk // 2, 1) * sin
            plain_s[:, lanes] = xr.astype(_BF16)
            scaled_s[:, lanes] = (xr * row_factor_ref[hd]).astype(_BF16)

    def head_slices(c, hd):
        return row_blocks[c], slice(hd * dk, (hd + 1) * dk), slice(hd * dv, (hd + 1) * dv)

    def mix_unit(c, g):
        rows = row_blocks[c]
        gl = slice(g * gd, (g + 1) * gd)
        mix_s[rows, gl] = _dot(wsc_ref[g], vn_s[rows, gl]) + bs_ref[g]

    units = [(c, hd) for c in range(n_chunks) for hd in range(RET_HEADS)]


    for hf in halves:
        x1_s[:, hf] = xl_ref[0, :, hf] + _wdot(mg_s[...], wo_ref[:, hf])
    h_s[...] = _rms(x_ref[0], gm_ref[...]).astype(_BF16)
    q_raw = proj(_QK0, 0)
    hp_s[...] = _rms(x1_s[...], gp_ref[...]).astype(_BF16)
    k_raw = proj(_QK0, 1)
    pg = [_wdot(hp_s[...], wpg_ref[:, hf]) for hf in halves]
    rotary(q_raw, rope(0), rope(1), xi_ref, q_s, qx_s)
    sv0 = proj(_SV0, 0)
    rotary(k_raw, rope(2), rope(3), zeta_ref, k_s, kz_s)
    v_s[:, halves[0]] = proj(_V0, 0).astype(_BF16)
    g_s[:, halves[0]] = _gelu(sv0)
    sv1 = proj(_SV0, 1)
    v_s[:, halves[1]] = proj(_V0, 1).astype(_BF16)
    pp = _wdot(pl_ref[0].astype(_BF16), wpp_ref[...])
    x2 = x1_s[...] + jnp.concatenate([jax.nn.sigmoid(g) for g in pg], axis=1) * pp
    o_ref[0] = _rms(x2, gf_ref[...])
    rg0 = proj(_RG0, 0)

    scores, kv = {}, {}
    for c, hd in units:
        rows, lanes, vl = head_slices(c, hd)
        scores[c, hd] = lax.dot_general(q_s[rows, lanes], k_s[rows, lanes],
                                        (((1,), (1,)), ((), ())),
                                        preferred_element_type=_F32)
    g_s[:, halves[1]] = _gelu(sv1)
    for c, hd in units:
        rows, lanes, vl = head_slices(c, hd)
        kv[c, hd] = lax.dot_general(kz_s[rows, lanes], v_s[rows, vl],
                                    (((0,), (0,)), ((), ())),
                                    preferred_element_type=_F32)
    rg1 = proj(_RG0, 1)
    gate_s[:, halves[0]] = _silu(rg0)
    prev = {}
    for hd in range(RET_HEADS):
        st = state_ref[hd]
        for c in range(n_chunks):
            prev[c, hd] = st
            st = kv[c, hd] + cdec_ref[hd] * st
        state_ref[hd] = st
    r = {}
    for c, hd in units:
        rows, lanes, vl = head_slices(c, hd)
        lhs = jnp.concatenate([(scores[c, hd] * decay_ref[hd]).astype(_BF16),
                               qx_s[rows, lanes]], axis=1)
        rhs = jnp.concatenate([v_s[rows, vl], prev[c, hd].astype(_BF16)], axis=0)
        r[c, hd] = _dot(lhs, rhs)
    gate_s[:, halves[1]] = _silu(rg1)

    su0 = proj(_SU0, 0)
    vn_s[...] = _unit(g_s[...], GN_EPS).astype(_BF16)
    su1 = proj(_SU0, 1)
    for c, hd in units:
        rows, lanes, vl = head_slices(c, hd)
        ret_s[rows, vl] = (_unit(r[c, hd], GN_EPS) * gate_s[rows, vl]).astype(_BF16)
    sg0 = proj(_SG0, 0)
    for c in range(n_chunks):
        for g in range(SGU_GROUPS):
            mix_unit(c, g)
    gu_s[:, halves[0]] = _gelu(su0)
    sg1 = proj(_SG0, 1)
    ro0 = _wdot(ret_s[...], wro_ref[:, halves[0]])
    gu_s[:, halves[1]] = _gelu(su1)
    ro1 = _wdot(ret_s[...], wro_ref[:, halves[1]])
    sgu_s[:, halves[0]] = (gu_s[:, halves[0]] * mix_s[:, halves[0]] * _silu(sg0)).astype(_BF16)
    mr0 = proj(_MR0, 0)
    sgu_s[:, halves[1]] = (gu_s[:, halves[1]] * mix_s[:, halves[1]] * _silu(sg1)).astype(_BF16)
    mr1 = proj(_MR0, 1)
    ms0 = proj(_MS0, 0)
    t_s[:, halves[0]] = jax.nn.sigmoid(mr0) * ro0
    ms1 = proj(_MS0, 1)
    t_s[:, halves[1]] = jax.nn.sigmoid(mr1) * ro1
    so0 = _wdot(sgu_s[...], wso_ref[:, halves[0]])
    sm0 = jax.nn.sigmoid(ms0)
    so1 = _wdot(sgu_s[...], wso_ref[:, halves[1]])
    sm1 = jax.nn.sigmoid(ms1)
    mg_s[:, halves[0]] = (t_s[:, halves[0]] + sm0 * so0).astype(_BF16)
    mg_s[:, halves[1]] = (t_s[:, halves[1]] + sm1 * so1).astype(_BF16)


def _position_tables(seq, ts):
    dk = RET_QK_DIM
    half = dk // 2
    inv = ROPE_BASE ** (-jnp.arange(half, dtype=_F32) / half)
    ang = jnp.arange(seq, dtype=_F32)[:, None] * inv[None, :]
    cos, sin = jnp.cos(ang), jnp.sin(ang)
    cos_full = jnp.concatenate([cos, cos], axis=-1)
    sin_signed = jnp.concatenate([-sin, sin], axis=-1)
    scale = dk ** -0.5

    log_g = jnp.log(1.0 - 2.0 ** (-5.0 - jnp.arange(RET_HEADS, dtype=_F32)))
    idx = jnp.arange(CHUNK, dtype=_F32)
    diff = idx[:, None] - idx[None, :]
    decay = jnp.where(diff[None] >= 0,
                      jnp.exp(jnp.maximum(diff, 0.0)[None] * log_g[:, None, None]), 0.0)
    zeta = jnp.exp((CHUNK - 1.0 - idx)[None, :] * log_g[:, None])
    xi = jnp.exp((idx + 1.0)[None, :] * log_g[:, None])
    cdec = jnp.exp(CHUNK * log_g)
    reps = ts // CHUNK
    tile = lambda t: jnp.broadcast_to(jnp.tile(t, (1, reps))[:, :, None],
                                      (RET_HEADS, ts, dk))
    rope = jnp.concatenate([cos_full * scale, sin_signed * scale, cos_full, sin_signed], axis=1)
    return (rope, tile(xi), tile(zeta), decay,
            jnp.broadcast_to(cdec[:, None, None], (RET_HEADS, 1, RET_V_DIM)))


def _const_spec(shape):
    zeros = (0,) * len(shape)
    return pl.BlockSpec(shape, lambda i: zeros, pipeline_mode=pl.Buffered(1))


@jax.jit
def kernel(x, p, w_in, w_ret_out, w_sgu_out, w_out, sgu_ws, sgu_bs, w_ple_gate,
           w_ple_proj, g_mixer, g_ple, g_final):
    batch, seq, d = x.shape
    depth = w_in.shape[0]
    ts = TOKENS_PER_STEP
    assert depth == 1 and d == D_MODEL and seq % ts == 0 and ts % CHUNK == 0
    nt = seq // ts
    n_tiles = batch * nt
    rope, xi, zeta, decay, cdec = _position_tables(seq, ts)
    tril = jnp.tril(jnp.ones((CHUNK, CHUNK), _F32))

    cur = lambda i: jnp.minimum(i, n_tiles - 1)
    lag = lambda i: jnp.maximum(i - 1, 0)
    cur_spec = lambda width: pl.BlockSpec(
        (1, ts, width), lambda i: (cur(i) // nt, cur(i) % nt, 0))
    lag_spec = lambda width: pl.BlockSpec(
        (1, ts, width), lambda i: (lag(i) // nt, lag(i) % nt, 0))

    ws_causal = (sgu_ws[0] * tril[None]).astype(_BF16)
    bs_rows = jnp.broadcast_to(sgu_bs[0][:, :, None],
                               (SGU_GROUPS, CHUNK, SGU_GROUP_DIM)).astype(_F32)
    consts = (rope, xi, zeta, decay, cdec,
              _pack_rows(w_in[0]), _pack_rows(w_ret_out[0]),
              _pack_rows(w_sgu_out[0]), _pack_rows(w_out[0]),
              ws_causal, bs_rows,
              _pack_rows(w_ple_gate[0]), _pack_rows(w_ple_proj[0]),
              g_mixer[0][None, :], g_ple[0][None, :], g_final[None, :])
    act = lambda width, dt: pltpu.VMEM((ts, width), dt)
    return pl.pallas_call(
        functools.partial(_block_kernel, nt, n_tiles + 1),
        grid=(n_tiles + 1,),
        in_specs=[cur_spec(D_MODEL), lag_spec(D_MODEL), lag_spec(PLE_DIM)]
                 + [_const_spec(c.shape) for c in consts],
        out_specs=lag_spec(D_MODEL),
        out_shape=jax.ShapeDtypeStruct(x.shape, x.dtype),
        scratch_shapes=[
            pltpu.VMEM((RET_HEADS, RET_QK_DIM, RET_V_DIM), _F32),
            act(D_MODEL, _BF16),
            act(RET_HEADS * RET_QK_DIM, _BF16),
            act(RET_HEADS * RET_QK_DIM, _BF16),
            act(RET_HEADS * RET_QK_DIM, _BF16),
            act(RET_HEADS * RET_QK_DIM, _BF16),
            act(D_MODEL, _BF16),
            act(D_MODEL, _F32),
            act(D_MODEL, _BF16),
            act(D_MODEL, _F32),
            act(D_MODEL, _BF16),
            act(D_MODEL, _F32),
            act(D_MODEL, _F32),
            act(D_MODEL, _BF16),
            act(D_MODEL, _F32),
            act(D_MODEL, _BF16),
            act(D_MODEL, _F32),
            act(D_MODEL, _BF16),
        ],
        compiler_params=pltpu.CompilerParams(
            dimension_semantics=("arbitrary",),
            vmem_limit_bytes=V7X_VMEM_LIMIT_BYTES),
        name="hybrid_block",
    )(x, x, p[0], *consts)
```

```python
import functools

import jax
import jax.numpy as jnp
from jax import lax
from jax.experimental import pallas as pl
from jax.experimental.pallas import tpu as pltpu

D_MODEL = 1024
PLE_DIM = 256
CHUNK = 128
RET_HEADS = 4
RET_V_DIM = D_MODEL // RET_HEADS
RET_QK_DIM = RET_V_DIM // 2
SGU_GROUPS = 4
SGU_GROUP_DIM = D_MODEL // SGU_GROUPS
ROPE_BASE = 10000.0
NORM_EPS = 1e-6
GN_EPS = 1e-5

_QK0 = 0
_V0 = 2 * RET_HEADS * RET_QK_DIM
_RG0 = _V0 + D_MODEL
_SU0 = _RG0 + D_MODEL
_SV0 = _SU0 + D_MODEL
_SG0 = _SV0 + D_MODEL
_MR0 = _SG0 + D_MODEL
_MS0 = _MR0 + D_MODEL
IN_WIDTH = _MS0 + D_MODEL

TOKENS_PER_STEP = 256
V7X_VMEM_LIMIT_BYTES = 56 * 1024 * 1024
PACK_BLOCK_ROWS = 256
PACK_BLOCK_COLS = 2048

_F32 = jnp.float32
_BF16 = jnp.bfloat16


def _dot(a, b):
    return jnp.dot(a, b, preferred_element_type=_F32)


def _pack_kernel(*refs):
    n = len(refs) // 2
    for w_ref, o_ref in zip(refs[:n], refs[n:]):
        o_ref[...] = pltpu.bitcast(w_ref[...].astype(_BF16), jnp.uint32)


def _pack_rows(*ws):
    n = ws[0].shape[1]
    nb = min(n, PACK_BLOCK_COLS)
    slabs = max(w.shape[0] for w in ws) // PACK_BLOCK_ROWS
    assert all(w.shape[1] == n and w.shape[0] % (2 * 8 * slabs) == 0 for w in ws) and n % nb == 0
    kbs = [w.shape[0] // slabs for w in ws]
    return pl.pallas_call(
        _pack_kernel,
        grid=(slabs, n // nb),
        in_specs=[pl.BlockSpec((kb, nb), lambda i, j: (i, j)) for kb in kbs],
        out_specs=[pl.BlockSpec((kb // 2, nb), lambda i, j: (i, j)) for kb in kbs],
        out_shape=[jax.ShapeDtypeStruct((w.shape[0] // 2, n), jnp.uint32) for w in ws],
        name="pack_weight_rows",
    )(*ws)


def _wdot(a, w_packed):
    return _dot(a, pltpu.bitcast(w_packed, _BF16))


def _rms(x, g):
    ms = jnp.mean(x * x, axis=-1, keepdims=True)
    return x * lax.rsqrt(ms + NORM_EPS) * g


def _unit(x, eps):
    mu = jnp.mean(x, axis=-1, keepdims=True)
    d = x - mu
    var = jnp.mean(d * d, axis=-1, keepdims=True)
    return d * lax.rsqrt(var + eps)


def _silu(x):
    return x * jax.nn.sigmoid(x)


def _gelu(x):
    return 0.5 * x * (1.0 + lax.erf(x * (0.5 ** 0.5)))


def _block_kernel(tiles_per_seq, n_steps,
                  x_ref, xl_ref, pl_ref, cq_ref, sq_ref, ck_ref, sk_ref, xi_ref, zeta_ref,
                  decay_ref, cdec_ref, win_ref, wro_ref, wso_ref, wo_ref, wsc_ref,
                  bs_ref, wpg_ref, wpp_ref, gm_ref, gp_ref, gf_ref,
                  o_ref,
                  state_ref, h_s, q_s, qx_s, k_s, kz_s, v_s, gate_s, ret_s, g_s, vn_s,
                  gu_s, mix_s, sgu_s, t_s, mg_s, x1_s, hp_s):
    ts = x_ref.shape[1]
    n_chunks = ts // CHUNK
    dk, dv, gd = RET_QK_DIM, RET_V_DIM, SGU_GROUP_DIM
    half = D_MODEL // 2
    halves = (slice(0, half), slice(half, D_MODEL))
    row_blocks = [slice(c * CHUNK, (c + 1) * CHUNK) for c in range(n_chunks)]
    step = pl.program_id(0)
    tile_in_seq = lax.rem(jnp.minimum(step, n_steps - 2), tiles_per_seq)
    pos_rows = pl.ds(pl.multiple_of(tile_in_seq * ts, ts), ts)

    @pl.when(step == 0)
    def _():
        mg_s[...] = jnp.zeros_like(mg_s)

    @pl.when(lax.rem(step, tiles_per_seq) == 0)
    def _():
        state_ref[...] = jnp.zeros_like(state_ref)

    def proj(col0, hf):
        c0 = col0 + hf * half
        return _wdot(h_s[...], win_ref[:, c0:c0 + half])

    def rotary(val, cos, sin, row_factor_ref, plain_s, scaled_s):
        for hd in range(RET_HEADS):
            lanes = slice(hd * dk, (hd + 1) * dk)
            xh = val[:, lanes]
            xr = xh * cos + pltpu.roll(xh, dk // 2, 1) * sin
            plain_s[:, lanes] = xr.astype(_BF16)
            scaled_s[:, lanes] = (xr * row_factor_ref[hd]).astype(_BF16)

    def head_slices(c, hd):
        return row_blocks[c], slice(hd * dk, (hd + 1) * dk), slice(hd * dv, (hd + 1) * dv)

    def mix_unit(c, g):
        rows = row_blocks[c]
        gl = slice(g * gd, (g + 1) * gd)
        mix_s[rows, gl] = _dot(wsc_ref[g], vn_s[rows, gl]) + bs_ref[g]

    units = [(c, hd) for c in range(n_chunks) for hd in range(RET_HEADS)]


    for hf in halves:
        x1_s[:, hf] = xl_ref[0, :, hf] + _wdot(mg_s[...], wo_ref[:, hf])
    h_s[...] = _rms(x_ref[0], gm_ref[...]).astype(_BF16)
    q_raw = proj(_QK0, 0)
    hp_s[...] = _rms(x1_s[...], gp_ref[...]).astype(_BF16)
    k_raw = proj(_QK0, 1)
    pg = [_wdot(hp_s[...], wpg_ref[:, hf]) for hf in halves]
    rotary(q_raw, cq_ref[pos_rows, :], sq_ref[pos_rows, :], xi_ref, q_s, qx_s)
    sv0 = proj(_SV0, 0)
    rotary(k_raw, ck_ref[pos_rows, :], sk_ref[pos_rows, :], zeta_ref, k_s, kz_s)
    v_s[:, halves[0]] = proj(_V0, 0).astype(_BF16)
    g_s[:, halves[0]] = _gelu(sv0)
    sv1 = proj(_SV0, 1)
    v_s[:, halves[1]] = proj(_V0, 1).astype(_BF16)
    pp = _wdot(pl_ref[0].astype(_BF16), wpp_ref[...])
    x2 = x1_s[...] + jnp.concatenate([jax.nn.sigmoid(g) for g in pg], axis=1) * pp
    o_ref[0] = _rms(x2, gf_ref[...])
    rg0 = proj(_RG0, 0)

    scores, kv = {}, {}
    for c, hd in units:
        rows, lanes, vl = head_slices(c, hd)
        scores[c, hd] = lax.dot_general(q_s[rows, lanes], k_s[rows, lanes],
                                        (((1,), (1,)), ((), ())),
                                        preferred_element_type=_F32)
    g_s[:, halves[1]] = _gelu(sv1)
    for c, hd in units:
        rows, lanes, vl = head_slices(c, hd)
        kv[c, hd] = lax.dot_general(kz_s[rows, lanes], v_s[rows, vl],
                                    (((0,), (0,)), ((), ())),
                                    preferred_element_type=_F32)
    rg1 = proj(_RG0, 1)
    gate_s[:, halves[0]] = _silu(rg0)
    prev = {}
    for hd in range(RET_HEADS):
        st = state_ref[hd]
        for c in range(n_chunks):
            prev[c, hd] = st
            st = kv[c, hd] + cdec_ref[hd] * st
        state_ref[hd] = st
    r = {}
    for c, hd in units:
        rows, lanes, vl = head_slices(c, hd)
        lhs = jnp.concatenate([(scores[c, hd] * decay_ref[hd]).astype(_BF16),
                               qx_s[rows, lanes]], axis=1)
        rhs = jnp.concatenate([v_s[rows, vl], prev[c, hd].astype(_BF16)], axis=0)
        r[c, hd] = _dot(lhs, rhs)
    gate_s[:, halves[1]] = _silu(rg1)

    su0 = proj(_SU0, 0)
    vn_s[...] = _unit(g_s[...], GN_EPS).astype(_BF16)
    su1 = proj(_SU0, 1)
    for c, hd in units:
        rows, lanes, vl = head_slices(c, hd)
        ret_s[rows, vl] = (_unit(r[c, hd], GN_EPS) * gate_s[rows, vl]).astype(_BF16)
    sg0 = proj(_SG0, 0)
    for c in range(n_chunks):
        for g in range(SGU_GROUPS):
            mix_unit(c, g)
    gu_s[:, halves[0]] = _gelu(su0)
    sg1 = proj(_SG0, 1)
    ro0 = _wdot(ret_s[...], wro_ref[:, halves[0]])
    gu_s[:, halves[1]] = _gelu(su1)
    ro1 = _wdot(ret_s[...], wro_ref[:, halves[1]])
    sgu_s[:, halves[0]] = (gu_s[:, halves[0]] * mix_s[:, halves[0]] * _silu(sg0)).astype(_BF16)
    mr0 = proj(_MR0, 0)
    sgu_s[:, halves[1]] = (gu_s[:, halves[1]] * mix_s[:, halves[1]] * _silu(sg1)).astype(_BF16)
    mr1 = proj(_MR0, 1)
    ms0 = proj(_MS0, 0)
    t_s[:, halves[0]] = jax.nn.sigmoid(mr0) * ro0
    ms1 = proj(_MS0, 1)
    t_s[:, halves[1]] = jax.nn.sigmoid(mr1) * ro1
    so0 = _wdot(sgu_s[...], wso_ref[:, halves[0]])
    sm0 = jax.nn.sigmoid(ms0)
    so1 = _wdot(sgu_s[...], wso_ref[:, halves[1]])
    sm1 = jax.nn.sigmoid(ms1)
    mg_s[:, halves[0]] = (t_s[:, halves[0]] + sm0 * so0).astype(_BF16)
    mg_s[:, halves[1]] = (t_s[:, halves[1]] + sm1 * so1).astype(_BF16)


def _position_tables(seq, ts):
    dk = RET_QK_DIM
    half = dk // 2
    inv = ROPE_BASE ** (-jnp.arange(half, dtype=_F32) / half)
    ang = jnp.arange(seq, dtype=_F32)[:, None] * inv[None, :]
    cos, sin = jnp.cos(ang), jnp.sin(ang)
    cos_full = jnp.concatenate([cos, cos], axis=-1)
    sin_signed = jnp.concatenate([-sin, sin], axis=-1)
    scale = dk ** -0.5

    log_g = jnp.log(1.0 - 2.0 ** (-5.0 - jnp.arange(RET_HEADS, dtype=_F32)))
    idx = jnp.arange(CHUNK, dtype=_F32)
    diff = idx[:, None] - idx[None, :]
    decay = jnp.where(diff[None] >= 0,
                      jnp.exp(jnp.maximum(diff, 0.0)[None] * log_g[:, None, None]), 0.0)
    zeta = jnp.exp((CHUNK - 1.0 - idx)[None, :] * log_g[:, None])
    xi = jnp.exp((idx + 1.0)[None, :] * log_g[:, None])
    cdec = jnp.exp(CHUNK * log_g)
    reps = ts // CHUNK
    tile = lambda t: jnp.broadcast_to(jnp.tile(t, (1, reps))[:, :, None],
                                      (RET_HEADS, ts, dk))
    return (cos_full * scale, sin_signed * scale, cos_full, sin_signed,
            tile(xi), tile(zeta), decay,
            jnp.broadcast_to(cdec[:, None, None], (RET_HEADS, 1, RET_V_DIM)))


def _const_spec(shape):
    zeros = (0,) * len(shape)
    return pl.BlockSpec(shape, lambda i: zeros, pipeline_mode=pl.Buffered(1))


@jax.jit
def kernel(x, p, w_in, w_ret_out, w_sgu_out, w_out, sgu_ws, sgu_bs, w_ple_gate,
           w_ple_proj, g_mixer, g_ple, g_final):
    batch, seq, d = x.shape
    depth = w_in.shape[0]
    ts = TOKENS_PER_STEP
    assert depth == 1 and d == D_MODEL and seq % ts == 0 and ts % CHUNK == 0
    nt = seq // ts
    n_tiles = batch * nt
    cq, sq, ck, sk, xi, zeta, decay, cdec = _position_tables(seq, ts)
    tril = jnp.tril(jnp.ones((CHUNK, CHUNK), _F32))

    x_tiles = x.reshape(n_tiles, ts, d)
    p_tiles = p[0].reshape(n_tiles, ts, PLE_DIM)
    cur_spec = lambda width: pl.BlockSpec(
        (1, ts, width), lambda i: (jnp.minimum(i, n_tiles - 1), 0, 0))
    lag_spec = lambda width: pl.BlockSpec(
        (1, ts, width), lambda i: (jnp.maximum(i - 1, 0), 0, 0))

    ws_causal = (sgu_ws[0] * tril[None]).astype(_BF16)
    bs_rows = jnp.broadcast_to(sgu_bs[0][:, :, None],
                               (SGU_GROUPS, CHUNK, SGU_GROUP_DIM)).astype(_F32)
    (win_p,) = _pack_rows(w_in[0])
    wro_p, wso_p, wo_p, wpg_p, wpp_p = _pack_rows(
        w_ret_out[0], w_sgu_out[0], w_out[0], w_ple_gate[0], w_ple_proj[0])
    consts = (cq, sq, ck, sk, xi, zeta, decay, cdec,
              win_p, wro_p, wso_p, wo_p, ws_causal, bs_rows, wpg_p, wpp_p,
              g_mixer[0][None, :], g_ple[0][None, :], g_final[None, :])
    act = lambda width, dt: pltpu.VMEM((ts, width), dt)
    out = pl.pallas_call(
        functools.partial(_block_kernel, nt, n_tiles + 1),
        grid=(n_tiles + 1,),
        in_specs=[cur_spec(D_MODEL), lag_spec(D_MODEL), lag_spec(PLE_DIM)]
                 + [_const_spec(c.shape) for c in consts],
        out_specs=lag_spec(D_MODEL),
        out_shape=jax.ShapeDtypeStruct(x_tiles.shape, x.dtype),
        scratch_shapes=[
            pltpu.VMEM((RET_HEADS, RET_QK_DIM, RET_V_DIM), _F32),
            act(D_MODEL, _BF16),
            act(RET_HEADS * RET_QK_DIM, _BF16),
            act(RET_HEADS * RET_QK_DIM, _BF16),
            act(RET_HEADS * RET_QK_DIM, _BF16),
            act(RET_HEADS * RET_QK_DIM, _BF16),
            act(D_MODEL, _BF16),
            act(D_MODEL, _F32),
            act(D_MODEL, _BF16),
            act(D_MODEL, _F32),
            act(D_MODEL, _BF16),
            act(D_MODEL, _F32),
            act(D_MODEL, _F32),
            act(D_MODEL, _BF16),
            act(D_MODEL, _F32),
            act(D_MODEL, _BF16),
            act(D_MODEL, _F32),
            act(D_MODEL, _BF16),
        ],
        compiler_params=pltpu.CompilerParams(
            dimension_semantics=("arbitrary",),
            vmem_limit_bytes=V7X_VMEM_LIMIT_BYTES),
        name="hybrid_block",
    )(x_tiles, x_tiles, p_tiles, *consts)
    return out.reshape(x.shape)
```

```python
import functools

import jax
import jax.numpy as jnp
from jax import lax
from jax.experimental import pallas as pl
from jax.experimental.pallas import tpu as pltpu

D_MODEL = 1024
PLE_DIM = 256
CHUNK = 128
RET_HEADS = 4
RET_V_DIM = D_MODEL // RET_HEADS
RET_QK_DIM = RET_V_DIM // 2
SGU_GROUPS = 4
SGU_GROUP_DIM = D_MODEL // SGU_GROUPS
ROPE_BASE = 10000.0
NORM_EPS = 1e-6
GN_EPS = 1e-5

_QK0 = 0
_V0 = 2 * RET_HEADS * RET_QK_DIM
_RG0 = _V0 + D_MODEL
_SU0 = _RG0 + D_MODEL
_SV0 = _SU0 + D_MODEL
_SG0 = _SV0 + D_MODEL
_MR0 = _SG0 + D_MODEL
_MS0 = _MR0 + D_MODEL
IN_WIDTH = _MS0 + D_MODEL

TOKENS_PER_STEP = 128
V7X_VMEM_LIMIT_BYTES = 56 * 1024 * 1024
PACK_BLOCK_ROWS = 256
PACK_BLOCK_COLS = 2048

_F32 = jnp.float32
_BF16 = jnp.bfloat16


def _dot(a, b):
    return jnp.dot(a, b, preferred_element_type=_F32)


def _pack_kernel(*refs):
    n = len(refs) // 2
    for w_ref, o_ref in zip(refs[:n], refs[n:]):
        o_ref[...] = pltpu.bitcast(w_ref[...].astype(_BF16), jnp.uint32)


def _pack_rows(*ws):
    n = ws[0].shape[1]
    nb = min(n, PACK_BLOCK_COLS)
    slabs = max(w.shape[0] for w in ws) // PACK_BLOCK_ROWS
    assert all(w.shape[1] == n and w.shape[0] % (2 * 8 * slabs) == 0 for w in ws) and n % nb == 0
    kbs = [w.shape[0] // slabs for w in ws]
    return pl.pallas_call(
        _pack_kernel,
        grid=(slabs, n // nb),
        in_specs=[pl.BlockSpec((kb, nb), lambda i, j: (i, j)) for kb in kbs],
        out_specs=[pl.BlockSpec((kb // 2, nb), lambda i, j: (i, j)) for kb in kbs],
        out_shape=[jax.ShapeDtypeStruct((w.shape[0] // 2, n), jnp.uint32) for w in ws],
        name="pack_weight_rows",
    )(*ws)


def _wdot(a, w_packed):
    return _dot(a, pltpu.bitcast(w_packed, _BF16))


def _rms(x, g):
    ms = jnp.mean(x * x, axis=-1, keepdims=True)
    return x * lax.rsqrt(ms + NORM_EPS) * g


def _unit(x, eps):
    mu = jnp.mean(x, axis=-1, keepdims=True)
    d = x - mu
    var = jnp.mean(d * d, axis=-1, keepdims=True)
    return d * lax.rsqrt(var + eps)


def _silu(x):
    return x * jax.nn.sigmoid(x)


def _gelu(x):
    return 0.5 * x * (1.0 + lax.erf(x * (0.5 ** 0.5)))


def _block_kernel(tiles_per_seq, n_steps,
                  x_ref, xl_ref, pl_ref, cq_ref, sq_ref, ck_ref, sk_ref, xi_ref, zeta_ref,
                  decay_ref, cdec_ref, win_ref, wro_ref, wso_ref, wo_ref, wsc_ref,
                  bs_ref, wpg_ref, wpp_ref, gm_ref, gp_ref, gf_ref,
                  o_ref,
                  state_ref, h_s, q_s, qx_s, k_s, kz_s, v_s, gate_s, ret_s, g_s, vn_s,
                  gu_s, mix_s, sgu_s, t_s, mg_s, x1_s, hp_s):
    ts = x_ref.shape[1]
    n_chunks = ts // CHUNK
    dk, dv, gd = RET_QK_DIM, RET_V_DIM, SGU_GROUP_DIM
    half = D_MODEL // 2
    halves = (slice(0, half), slice(half, D_MODEL))
    row_blocks = [slice(c * CHUNK, (c + 1) * CHUNK) for c in range(n_chunks)]
    step = pl.program_id(0)
    tile_in_seq = lax.rem(jnp.minimum(step, n_steps - 2), tiles_per_seq)
    pos_rows = pl.ds(pl.multiple_of(tile_in_seq * ts, ts), ts)

    @pl.when(step == 0)
    def _():
        mg_s[...] = jnp.zeros_like(mg_s)

    @pl.when(lax.rem(step, tiles_per_seq) == 0)
    def _():
        state_ref[...] = jnp.zeros_like(state_ref)

    def proj(col0, hf):
        c0 = col0 + hf * half
        return _wdot(h_s[...], win_ref[:, c0:c0 + half])

    def rotary(val, cos, sin, row_factor_ref, plain_s, scaled_s):
        for hd in range(RET_HEADS):
            lanes = slice(hd * dk, (hd + 1) * dk)
            xh = val[:, lanes]
            xr = xh * cos + pltpu.roll(xh, dk // 2, 1) * sin
            plain_s[:, lanes] = xr.astype(_BF16)
            scaled_s[:, lanes] = (xr * row_factor_ref[hd]).astype(_BF16)

    def head_slices(c, hd):
        return row_blocks[c], slice(hd * dk, (hd + 1) * dk), slice(hd * dv, (hd + 1) * dv)

    def mix_unit(c, g):
        rows = row_blocks[c]
        gl = slice(g * gd, (g + 1) * gd)
        mix_s[rows, gl] = _dot(wsc_ref[g], vn_s[rows, gl]) + bs_ref[g]

    units = [(c, hd) for c in range(n_chunks) for hd in range(RET_HEADS)]


    for hf in halves:
        x1_s[:, hf] = xl_ref[0, :, hf] + _wdot(mg_s[...], wo_ref[:, hf])
    h_s[...] = _rms(x_ref[0], gm_ref[...]).astype(_BF16)
    q_raw = proj(_QK0, 0)
    hp_s[...] = _rms(x1_s[...], gp_ref[...]).astype(_BF16)
    k_raw = proj(_QK0, 1)
    pg = [_wdot(hp_s[...], wpg_ref[:, hf]) for hf in halves]
    rotary(q_raw, cq_ref[pos_rows, :], sq_ref[pos_rows, :], xi_ref, q_s, qx_s)
    sv0 = proj(_SV0, 0)
    rotary(k_raw, ck_ref[pos_rows, :], sk_ref[pos_rows, :], zeta_ref, k_s, kz_s)
    v_s[:, halves[0]] = proj(_V0, 0).astype(_BF16)
    g_s[:, halves[0]] = _gelu(sv0)
    sv1 = proj(_SV0, 1)
    v_s[:, halves[1]] = proj(_V0, 1).astype(_BF16)
    pp = _wdot(pl_ref[0].astype(_BF16), wpp_ref[...])
    x2 = x1_s[...] + jnp.concatenate([jax.nn.sigmoid(g) for g in pg], axis=1) * pp
    o_ref[0] = _rms(x2, gf_ref[...])
    rg0 = proj(_RG0, 0)

    scores, kv = {}, {}
    for c, hd in units:
        rows, lanes, vl = head_slices(c, hd)
        scores[c, hd] = lax.dot_general(q_s[rows, lanes], k_s[rows, lanes],
                                        (((1,), (1,)), ((), ())),
                                        preferred_element_type=_F32)
    g_s[:, halves[1]] = _gelu(sv1)
    for c, hd in units:
        rows, lanes, vl = head_slices(c, hd)
        kv[c, hd] = lax.dot_general(kz_s[rows, lanes], v_s[rows, vl],
                                    (((0,), (0,)), ((), ())),
                                    preferred_element_type=_F32)
    rg1 = proj(_RG0, 1)
    gate_s[:, halves[0]] = _silu(rg0)
    prev = {}
    for hd in range(RET_HEADS):
        st = state_ref[hd]
        for c in range(n_chunks):
            prev[c, hd] = st
            st = kv[c, hd] + cdec_ref[hd] * st
        state_ref[hd] = st
    r = {}
    for c, hd in units:
        rows, lanes, vl = head_slices(c, hd)
        lhs = jnp.concatenate([(scores[c, hd] * decay_ref[hd]).astype(_BF16),
                               qx_s[rows, lanes]], axis=1)
        rhs = jnp.concatenate([v_s[rows, vl], prev[c, hd].astype(_BF16)], axis=0)
        r[c, hd] = _dot(lhs, rhs)
    gate_s[:, halves[1]] = _silu(rg1)

    su0 = proj(_SU0, 0)
    vn_s[...] = _unit(g_s[...], GN_EPS).astype(_BF16)
    su1 = proj(_SU0, 1)
    for c, hd in units:
        rows, lanes, vl = head_slices(c, hd)
        ret_s[rows, vl] = (_unit(r[c, hd], GN_EPS) * gate_s[rows, vl]).astype(_BF16)
    sg0 = proj(_SG0, 0)
    for c in range(n_chunks):
        for g in range(SGU_GROUPS):
            mix_unit(c, g)
    gu_s[:, halves[0]] = _gelu(su0)
    sg1 = proj(_SG0, 1)
    ro0 = _wdot(ret_s[...], wro_ref[:, halves[0]])
    gu_s[:, halves[1]] = _gelu(su1)
    ro1 = _wdot(ret_s[...], wro_ref[:, halves[1]])
    sgu_s[:, halves[0]] = (gu_s[:, halves[0]] * mix_s[:, halves[0]] * _silu(sg0)).astype(_BF16)
    mr0 = proj(_MR0, 0)
    sgu_s[:, halves[1]] = (gu_s[:, halves[1]] * mix_s[:, halves[1]] * _silu(sg1)).astype(_BF16)
    mr1 = proj(_MR0, 1)
    ms0 = proj(_MS0, 0)
    t_s[:, halves[0]] = jax.nn.sigmoid(mr0) * ro0
    ms1 = proj(_MS0, 1)
    t_s[:, halves[1]] = jax.nn.sigmoid(mr1) * ro1
    so0 = _wdot(sgu_s[...], wso_ref[:, halves[0]])
    sm0 = jax.nn.sigmoid(ms0)
    so1 = _wdot(sgu_s[...], wso_ref[:, halves[1]])
    sm1 = jax.nn.sigmoid(ms1)
    mg_s[:, halves[0]] = (t_s[:, halves[0]] + sm0 * so0).astype(_BF16)
    mg_s[:, halves[1]] = (t_s[:, halves[1]] + sm1 * so1).astype(_BF16)


def _position_tables(seq, ts):
    dk = RET_QK_DIM
    half = dk // 2
    inv = ROPE_BASE ** (-jnp.arange(half, dtype=_F32) / half)
    ang = jnp.arange(seq, dtype=_F32)[:, None] * inv[None, :]
    cos, sin = jnp.cos(ang), jnp.sin(ang)
    cos_full = jnp.concatenate([cos, cos], axis=-1)
    sin_signed = jnp.concatenate([-sin, sin], axis=-1)
    scale = dk ** -0.5

    log_g = jnp.log(1.0 - 2.0 ** (-5.0 - jnp.arange(RET_HEADS, dtype=_F32)))
    idx = jnp.arange(CHUNK, dtype=_F32)
    diff = idx[:, None] - idx[None, :]
    decay = jnp.where(diff[None] >= 0,
                      jnp.exp(jnp.maximum(diff, 0.0)[None] * log_g[:, None, None]), 0.0)
    zeta = jnp.exp((CHUNK - 1.0 - idx)[None, :] * log_g[:, None])
    xi = jnp.exp((idx + 1.0)[None, :] * log_g[:, None])
    cdec = jnp.exp(CHUNK * log_g)
    reps = ts // CHUNK
    tile = lambda t: jnp.broadcast_to(jnp.tile(t, (1, reps))[:, :, None],
                                      (RET_HEADS, ts, dk))
    return (cos_full * scale, sin_signed * scale, cos_full, sin_signed,
            tile(xi), tile(zeta), decay,
            jnp.broadcast_to(cdec[:, None, None], (RET_HEADS, 1, RET_V_DIM)))


def _const_spec(shape):
    zeros = (0,) * len(shape)
    return pl.BlockSpec(shape, lambda i: zeros, pipeline_mode=pl.Buffered(1))


@jax.jit
def kernel(x, p, w_in, w_ret_out, w_sgu_out, w_out, sgu_ws, sgu_bs, w_ple_gate,
           w_ple_proj, g_mixer, g_ple, g_final):
    batch, seq, d = x.shape
    depth = w_in.shape[0]
    ts = TOKENS_PER_STEP
    assert depth == 1 and d == D_MODEL and seq % ts == 0 and ts % CHUNK == 0
    nt = seq // ts
    n_tiles = batch * nt
    cq, sq, ck, sk, xi, zeta, decay, cdec = _position_tables(seq, ts)
    tril = jnp.tril(jnp.ones((CHUNK, CHUNK), _F32))

    x_tiles = x.reshape(n_tiles, ts, d)
    p_tiles = p[0].reshape(n_tiles, ts, PLE_DIM)
    cur_spec = lambda width: pl.BlockSpec(
        (1, ts, width), lambda i: (jnp.minimum(i, n_tiles - 1), 0, 0))
    lag_spec = lambda width: pl.BlockSpec(
        (1, ts, width), lambda i: (jnp.maximum(i - 1, 0), 0, 0))

    ws_causal = (sgu_ws[0] * tril[None]).astype(_BF16)
    bs_rows = jnp.broadcast_to(sgu_bs[0][:, :, None],
                               (SGU_GROUPS, CHUNK, SGU_GROUP_DIM)).astype(_F32)
    (win_p,) = _pack_rows(w_in[0])
    wro_p, wso_p, wo_p, wpg_p, wpp_p = _pack_rows(
        w_ret_out[0], w_sgu_out[0], w_out[0], w_ple_gate[0], w_ple_proj[0])
    consts = (cq, sq, ck, sk, xi, zeta, decay, cdec,
              win_p, wro_p, wso_p, wo_p, ws_causal, bs_rows, wpg_p, wpp_p,
              g_mixer[0][None, :], g_ple[0][None, :], g_final[None, :])
    act = lambda width, dt: pltpu.VMEM((ts, width), dt)
    out = pl.pallas_call(
        functools.partial(_block_kernel, nt, n_tiles + 1),
        grid=(n_tiles + 1,),
        in_specs=[cur_spec(D_MODEL), lag_spec(D_MODEL), lag_spec(PLE_DIM)]
                 + [_const_spec(c.shape) for c in consts],
        out_specs=lag_spec(D_MODEL),
        out_shape=jax.ShapeDtypeStruct(x_tiles.shape, x.dtype),
        scratch_shapes=[
            pltpu.VMEM((RET_HEADS, RET_QK_DIM, RET_V_DIM), _F32),
            act(D_MODEL, _BF16),
            act(RET_HEADS * RET_QK_DIM, _BF16),
            act(RET_HEADS * RET_QK_DIM, _BF16),
            act(RET_HEADS * RET_QK_DIM, _BF16),
            act(RET_HEADS * RET_QK_DIM, _BF16),
            act(D_MODEL, _BF16),
            act(D_MODEL, _F32),
            act(D_MODEL, _BF16),
            act(D_MODEL, _F32),
            act(D_MODEL, _BF16),
            act(D_MODEL, _F32),
            act(D_MODEL, _F32),
            act(D_MODEL, _BF16),
            act(D_MODEL, _F32),
            act(D_MODEL, _BF16),
            act(D_MODEL, _F32),
            act(D_MODEL, _BF16),
        ],
        compiler_params=pltpu.CompilerParams(
            dimension_semantics=("arbitrary",),
            vmem_limit_bytes=V7X_VMEM_LIMIT_BYTES),
        name="hybrid_block",
    )(x_tiles, x_tiles, p_tiles, *consts)
    return out.reshape(x.shape)
```

```python
import functools

import jax
import jax.numpy as jnp
from jax import lax
from jax.experimental import pallas as pl
from jax.experimental.pallas import tpu as pltpu

D_MODEL = 1024
PLE_DIM = 256
CHUNK = 128
RET_HEADS = 4
RET_V_DIM = D_MODEL // RET_HEADS
RET_QK_DIM = RET_V_DIM // 2
SGU_GROUPS = 4
SGU_GROUP_DIM = D_MODEL // SGU_GROUPS
ROPE_BASE = 10000.0
NORM_EPS = 1e-6
GN_EPS = 1e-5

_QK0 = 0
_V0 = 2 * RET_HEADS * RET_QK_DIM
_RG0 = _V0 + D_MODEL
_SU0 = _RG0 + D_MODEL
_SV0 = _SU0 + D_MODEL
_SG0 = _SV0 + D_MODEL
_MR0 = _SG0 + D_MODEL
_MS0 = _MR0 + D_MODEL
IN_WIDTH = _MS0 + D_MODEL

TOKENS_PER_TILE = 256
TILES_PER_STEP = 2
V7X_VMEM_LIMIT_BYTES = 60 * 1024 * 1024
PACK_BLOCK_ROWS = 256
PACK_BLOCK_COLS = 2048

_F32 = jnp.float32
_BF16 = jnp.bfloat16


def _dot(a, b):
    return jnp.dot(a, b, preferred_element_type=_F32)


def _pack_kernel(*refs):
    n = len(refs) // 2
    for w_ref, o_ref in zip(refs[:n], refs[n:]):
        o_ref[...] = pltpu.bitcast(w_ref[...].astype(_BF16), jnp.uint32)


def _pack_rows(*ws):
    n = ws[0].shape[1]
    nb = min(n, PACK_BLOCK_COLS)
    slabs = max(w.shape[0] for w in ws) // PACK_BLOCK_ROWS
    assert all(w.shape[1] == n and w.shape[0] % (2 * 8 * slabs) == 0 for w in ws) and n % nb == 0
    kbs = [w.shape[0] // slabs for w in ws]
    return pl.pallas_call(
        _pack_kernel,
        grid=(slabs, n // nb),
        in_specs=[pl.BlockSpec((kb, nb), lambda i, j: (i, j)) for kb in kbs],
        out_specs=[pl.BlockSpec((kb // 2, nb), lambda i, j: (i, j)) for kb in kbs],
        out_shape=[jax.ShapeDtypeStruct((w.shape[0] // 2, n), jnp.uint32) for w in ws],
        name="pack_weight_rows",
    )(*ws)


def _wdot(a, w_packed):
    return _dot(a, pltpu.bitcast(w_packed, _BF16))


def _rms(x, g):
    ms = jnp.mean(x * x, axis=-1, keepdims=True)
    return x * lax.rsqrt(ms + NORM_EPS) * g


def _unit(x, eps):
    mu = jnp.mean(x, axis=-1, keepdims=True)
    d = x - mu
    var = jnp.mean(d * d, axis=-1, keepdims=True)
    return d * lax.rsqrt(var + eps)


def _silu(x):
    return x * jax.nn.sigmoid(x)


def _gelu(x):
    return 0.5 * x * (1.0 + lax.erf(x * (0.5 ** 0.5)))


def _block_kernel(tiles_per_seq, n_blocks,
                  x_ref, xl_ref, pl_ref, cq_ref, sq_ref, ck_ref, sk_ref, xi_ref, zeta_ref,
                  decay_ref, cdec_ref, win_ref, wro_ref, wso_ref, wo_ref, wsc_ref,
                  bs_ref, wpg_ref, wpp_ref, gm_ref, gp_ref, gf_ref,
                  o_ref,
                  state_ref, h_s, q_s, qx_s, k_s, kz_s, v_s, gate_s, ret_s, g_s, vn_s,
                  gu_s, mix_s, sgu_s, t_s, mg_s, x1_s, hp_s):
    ts = TOKENS_PER_TILE
    tiles_per_block = x_ref.shape[1] // ts
    n_chunks = ts // CHUNK
    dk, dv, gd = RET_QK_DIM, RET_V_DIM, SGU_GROUP_DIM
    half = D_MODEL // 2
    halves = (slice(0, half), slice(half, D_MODEL))
    row_blocks = [slice(c * CHUNK, (c + 1) * CHUNK) for c in range(n_chunks)]
    block = pl.program_id(0)

    @pl.when(block == 0)
    def _():
        mg_s[...] = jnp.zeros_like(mg_s)

    def tile_body(j, carry):
        tile = jnp.minimum(block, n_blocks - 1) * tiles_per_block + j
        tile_in_seq = lax.rem(tile, tiles_per_seq)
        rows_j = pl.ds(pl.multiple_of(j * ts, ts), ts)
        pos_rows = pl.ds(pl.multiple_of(tile_in_seq * ts, ts), ts)

        @pl.when(tile_in_seq == 0)
        def _():
            state_ref[...] = jnp.zeros_like(state_ref)

        def proj(col0, hf):
            c0 = col0 + hf * half
            return _wdot(h_s[...], win_ref[:, c0:c0 + half])

        def rotary(val, cos, sin, row_factor_ref, plain_s, scaled_s):
            for hd in range(RET_HEADS):
                lanes = slice(hd * dk, (hd + 1) * dk)
                xh = val[:, lanes]
                xr = xh * cos + pltpu.roll(xh, dk // 2, 1) * sin
                plain_s[:, lanes] = xr.astype(_BF16)
                scaled_s[:, lanes] = (xr * row_factor_ref[hd]).astype(_BF16)

        def head_slices(c, hd):
            return row_blocks[c], slice(hd * dk, (hd + 1) * dk), slice(hd * dv, (hd + 1) * dv)

        def mix_unit(c, g):
            rows = row_blocks[c]
            gl = slice(g * gd, (g + 1) * gd)
            mix_s[rows, gl] = _dot(wsc_ref[g], vn_s[rows, gl]) + bs_ref[g]

        units = [(c, hd) for c in range(n_chunks) for hd in range(RET_HEADS)]


        for hf in halves:
            x1_s[:, hf] = xl_ref[0, rows_j, hf] + _wdot(mg_s[j], wo_ref[:, hf])
        h_s[...] = _rms(x_ref[0, rows_j, :], gm_ref[...]).astype(_BF16)
        q_raw = proj(_QK0, 0)
        hp_s[...] = _rms(x1_s[...], gp_ref[...]).astype(_BF16)
        k_raw = proj(_QK0, 1)
        pg = [_wdot(hp_s[...], wpg_ref[:, hf]) for hf in halves]
        rotary(q_raw, cq_ref[pos_rows, :], sq_ref[pos_rows, :], xi_ref, q_s, qx_s)
        sv0 = proj(_SV0, 0)
        rotary(k_raw, ck_ref[pos_rows, :], sk_ref[pos_rows, :], zeta_ref, k_s, kz_s)
        v_s[:, halves[0]] = proj(_V0, 0).astype(_BF16)
        g_s[:, halves[0]] = _gelu(sv0)
        sv1 = proj(_SV0, 1)
        v_s[:, halves[1]] = proj(_V0, 1).astype(_BF16)
        pp = _wdot(pl_ref[0, rows_j, :].astype(_BF16), wpp_ref[...])
        x2 = x1_s[...] + jnp.concatenate([jax.nn.sigmoid(g) for g in pg], axis=1) * pp
        o_ref[0, rows_j, :] = _rms(x2, gf_ref[...])
        rg0 = proj(_RG0, 0)

        scores, kv = {}, {}
        for c, hd in units:
            rows, lanes, vl = head_slices(c, hd)
            scores[c, hd] = lax.dot_general(q_s[rows, lanes], k_s[rows, lanes],
                                            (((1,), (1,)), ((), ())),
                                            preferred_element_type=_F32)
        g_s[:, halves[1]] = _gelu(sv1)
        for c, hd in units:
            rows, lanes, vl = head_slices(c, hd)
            kv[c, hd] = lax.dot_general(kz_s[rows, lanes], v_s[rows, vl],
                                        (((0,), (0,)), ((), ())),
                                        preferred_element_type=_F32)
        rg1 = proj(_RG0, 1)
        gate_s[:, halves[0]] = _silu(rg0)
        prev = {}
        for hd in range(RET_HEADS):
            st = state_ref[hd]
            for c in range(n_chunks):
                prev[c, hd] = st
                st = kv[c, hd] + cdec_ref[hd] * st
            state_ref[hd] = st
        r = {}
        for c, hd in units:
            rows, lanes, vl = head_slices(c, hd)
            lhs = jnp.concatenate([(scores[c, hd] * decay_ref[hd]).astype(_BF16),
                                   qx_s[rows, lanes]], axis=1)
            rhs = jnp.concatenate([v_s[rows, vl], prev[c, hd].astype(_BF16)], axis=0)
            r[c, hd] = _dot(lhs, rhs)
        gate_s[:, halves[1]] = _silu(rg1)

        su0 = proj(_SU0, 0)
        vn_s[...] = _unit(g_s[...], GN_EPS).astype(_BF16)
        su1 = proj(_SU0, 1)
        for c, hd in units:
            rows, lanes, vl = head_slices(c, hd)
            ret_s[rows, vl] = (_unit(r[c, hd], GN_EPS) * gate_s[rows, vl]).astype(_BF16)
        sg0 = proj(_SG0, 0)
        for c in range(n_chunks):
            for g in range(SGU_GROUPS):
                mix_unit(c, g)
        gu_s[:, halves[0]] = _gelu(su0)
        sg1 = proj(_SG0, 1)
        ro0 = _wdot(ret_s[...], wro_ref[:, halves[0]])
        gu_s[:, halves[1]] = _gelu(su1)
        ro1 = _wdot(ret_s[...], wro_ref[:, halves[1]])
        sgu_s[:, halves[0]] = (gu_s[:, halves[0]] * mix_s[:, halves[0]] * _silu(sg0)).astype(_BF16)
        mr0 = proj(_MR0, 0)
        sgu_s[:, halves[1]] = (gu_s[:, halves[1]] * mix_s[:, halves[1]] * _silu(sg1)).astype(_BF16)
        mr1 = proj(_MR0, 1)
        ms0 = proj(_MS0, 0)
        t_s[:, halves[0]] = jax.nn.sigmoid(mr0) * ro0
        ms1 = proj(_MS0, 1)
        t_s[:, halves[1]] = jax.nn.sigmoid(mr1) * ro1
        so0 = _wdot(sgu_s[...], wso_ref[:, halves[0]])
        sm0 = jax.nn.sigmoid(ms0)
        so1 = _wdot(sgu_s[...], wso_ref[:, halves[1]])
        sm1 = jax.nn.sigmoid(ms1)
        mg_s[j, :, halves[0]] = (t_s[:, halves[0]] + sm0 * so0).astype(_BF16)
        mg_s[j, :, halves[1]] = (t_s[:, halves[1]] + sm1 * so1).astype(_BF16)
        return carry

    lax.fori_loop(0, tiles_per_block, tile_body, 0)


def _position_tables(seq, ts):
    dk = RET_QK_DIM
    half = dk // 2
    inv = ROPE_BASE ** (-jnp.arange(half, dtype=_F32) / half)
    ang = jnp.arange(seq, dtype=_F32)[:, None] * inv[None, :]
    cos, sin = jnp.cos(ang), jnp.sin(ang)
    cos_full = jnp.concatenate([cos, cos], axis=-1)
    sin_signed = jnp.concatenate([-sin, sin], axis=-1)
    scale = dk ** -0.5

    log_g = jnp.log(1.0 - 2.0 ** (-5.0 - jnp.arange(RET_HEADS, dtype=_F32)))
    idx = jnp.arange(CHUNK, dtype=_F32)
    diff = idx[:, None] - idx[None, :]
    decay = jnp.where(diff[None] >= 0,
                      jnp.exp(jnp.maximum(diff, 0.0)[None] * log_g[:, None, None]), 0.0)
    zeta = jnp.exp((CHUNK - 1.0 - idx)[None, :] * log_g[:, None])
    xi = jnp.exp((idx + 1.0)[None, :] * log_g[:, None])
    cdec = jnp.exp(CHUNK * log_g)
    reps = ts // CHUNK
    tile = lambda t: jnp.broadcast_to(jnp.tile(t, (1, reps))[:, :, None],
                                      (RET_HEADS, ts, dk))
    return (cos_full * scale, sin_signed * scale, cos_full, sin_signed,
            tile(xi), tile(zeta), decay,
            jnp.broadcast_to(cdec[:, None, None], (RET_HEADS, 1, RET_V_DIM)))


def _const_spec(shape):
    zeros = (0,) * len(shape)
    return pl.BlockSpec(shape, lambda i: zeros, pipeline_mode=pl.Buffered(1))


@jax.jit
def kernel(x, p, w_in, w_ret_out, w_sgu_out, w_out, sgu_ws, sgu_bs, w_ple_gate,
           w_ple_proj, g_mixer, g_ple, g_final):
    batch, seq, d = x.shape
    depth = w_in.shape[0]
    ts = TOKENS_PER_TILE
    bt = ts * TILES_PER_STEP
    assert depth == 1 and d == D_MODEL and seq % bt == 0 and ts % CHUNK == 0
    n_blocks = batch * seq // bt
    cq, sq, ck, sk, xi, zeta, decay, cdec = _position_tables(seq, ts)
    tril = jnp.tril(jnp.ones((CHUNK, CHUNK), _F32))

    x_blocks = x.reshape(n_blocks, bt, d)
    p_blocks = p[0].reshape(n_blocks, bt, PLE_DIM)
    cur_spec = lambda width: pl.BlockSpec(
        (1, bt, width), lambda b: (jnp.minimum(b, n_blocks - 1), 0, 0))
    lag_spec = lambda width: pl.BlockSpec(
        (1, bt, width), lambda b: (jnp.maximum(b - 1, 0), 0, 0))

    ws_causal = (sgu_ws[0] * tril[None]).astype(_BF16)
    bs_rows = jnp.broadcast_to(sgu_bs[0][:, :, None],
                               (SGU_GROUPS, CHUNK, SGU_GROUP_DIM)).astype(_F32)
    (win_p,) = _pack_rows(w_in[0])
    wro_p, wso_p, wo_p, wpg_p, wpp_p = _pack_rows(
        w_ret_out[0], w_sgu_out[0], w_out[0], w_ple_gate[0], w_ple_proj[0])
    consts = (cq, sq, ck, sk, xi, zeta, decay, cdec,
              win_p, wro_p, wso_p, wo_p, ws_causal, bs_rows, wpg_p, wpp_p,
              g_mixer[0][None, :], g_ple[0][None, :], g_final[None, :])
    act = lambda width, dt: pltpu.VMEM((ts, width), dt)
    out = pl.pallas_call(
        functools.partial(_block_kernel, seq // ts, n_blocks),
        grid=(n_blocks + 1,),
        in_specs=[cur_spec(D_MODEL), lag_spec(D_MODEL), lag_spec(PLE_DIM)]
                 + [_const_spec(c.shape) for c in consts],
        out_specs=lag_spec(D_MODEL),
        out_shape=jax.ShapeDtypeStruct(x_blocks.shape, x.dtype),
        scratch_shapes=[
            pltpu.VMEM((RET_HEADS, RET_QK_DIM, RET_V_DIM), _F32),
            act(D_MODEL, _BF16),
            act(RET_HEADS * RET_QK_DIM, _BF16),
            act(RET_HEADS * RET_QK_DIM, _BF16),
            act(RET_HEADS * RET_QK_DIM, _BF16),
            act(RET_HEADS * RET_QK_DIM, _BF16),
            act(D_MODEL, _BF16),
            act(D_MODEL, _F32),
            act(D_MODEL, _BF16),
            act(D_MODEL, _F32),
            act(D_MODEL, _BF16),
            act(D_MODEL, _F32),
            act(D_MODEL, _F32),
            act(D_MODEL, _BF16),
            act(D_MODEL, _F32),
            pltpu.VMEM((TILES_PER_STEP, ts, D_MODEL), _BF16),
            act(D_MODEL, _F32),
            act(D_MODEL, _BF16),
        ],
        compiler_params=pltpu.CompilerParams(
            dimension_semantics=("arbitrary",),
            vmem_limit_bytes=V7X_VMEM_LIMIT_BYTES),
        name="hybrid_block",
    )(x_blocks, x_blocks, p_blocks, *consts)
    return out.reshape(x.shape)
```

```python
import functools

import jax
import jax.numpy as jnp
from jax import lax
from jax.experimental import pallas as pl
from jax.experimental.pallas import tpu as pltpu

D_MODEL = 1024
PLE_DIM = 256
CHUNK = 128
RET_HEADS = 4
RET_V_DIM = D_MODEL // RET_HEADS
RET_QK_DIM = RET_V_DIM // 2
SGU_GROUPS = 4
SGU_GROUP_DIM = D_MODEL // SGU_GROUPS
ROPE_BASE = 10000.0
NORM_EPS = 1e-6
GN_EPS = 1e-5

_QK0 = 0
_V0 = 2 * RET_HEADS * RET_QK_DIM
_RG0 = _V0 + D_MODEL
_SU0 = _RG0 + D_MODEL
_SV0 = _SU0 + D_MODEL
_SG0 = _SV0 + D_MODEL
_MR0 = _SG0 + D_MODEL
_MS0 = _MR0 + D_MODEL
IN_WIDTH = _MS0 + D_MODEL

TOKENS_PER_TILE = 256
TILES_PER_STEP = 2
V7X_VMEM_LIMIT_BYTES = 60 * 1024 * 1024
PACK_BLOCK_ROWS = 256
PACK_BLOCK_COLS = 2048

_F32 = jnp.float32
_BF16 = jnp.bfloat16


def _dot(a, b):
    return jnp.dot(a, b, preferred_element_type=_F32)


def _pack_kernel(*refs):
    n = len(refs) // 2
    for w_ref, o_ref in zip(refs[:n], refs[n:]):
        o_ref[...] = pltpu.bitcast(w_ref[...].astype(_BF16), jnp.uint32)


def _pack_rows(*ws):
    n = ws[0].shape[1]
    nb = min(n, PACK_BLOCK_COLS)
    slabs = max(w.shape[0] for w in ws) // PACK_BLOCK_ROWS
    assert all(w.shape[1] == n and w.shape[0] % (2 * 8 * slabs) == 0 for w in ws) and n % nb == 0
    kbs = [w.shape[0] // slabs for w in ws]
    return pl.pallas_call(
        _pack_kernel,
        grid=(slabs, n // nb),
        in_specs=[pl.BlockSpec((kb, nb), lambda i, j: (i, j)) for kb in kbs],
        out_specs=[pl.BlockSpec((kb // 2, nb), lambda i, j: (i, j)) for kb in kbs],
        out_shape=[jax.ShapeDtypeStruct((w.shape[0] // 2, n), jnp.uint32) for w in ws],
        name="pack_weight_rows",
    )(*ws)


def _wdot(a, w_packed):
    return _dot(a, pltpu.bitcast(w_packed, _BF16))


def _rms(x, g):
    ms = jnp.mean(x * x, axis=-1, keepdims=True)
    return x * lax.rsqrt(ms + NORM_EPS) * g


def _unit(x, eps):
    mu = jnp.mean(x, axis=-1, keepdims=True)
    d = x - mu
    var = jnp.mean(d * d, axis=-1, keepdims=True)
    return d * lax.rsqrt(var + eps)


def _silu(x):
    return x * jax.nn.sigmoid(x)


def _gelu(x):
    return 0.5 * x * (1.0 + lax.erf(x * (0.5 ** 0.5)))


def _block_kernel(tiles_per_seq, n_blocks,
                  x_ref, xl_ref, pl_ref, cq_ref, sq_ref, ck_ref, sk_ref, xi_ref, zeta_ref,
                  decay_ref, cdec_ref, win_ref, wro_ref, wso_ref, wo_ref, wsc_ref,
                  bs_ref, wpg_ref, wpp_ref, gm_ref, gp_ref, gf_ref,
                  o_ref,
                  state_ref, h_s, q_s, qx_s, k_s, kz_s, v_s, gate_s, ret_s, g_s, vn_s,
                  gu_s, mix_s, sgu_s, t_s, mg_s, x1_s, hp_s):
    ts = TOKENS_PER_TILE
    tiles_per_block = x_ref.shape[1] // ts
    n_chunks = ts // CHUNK
    dk, dv, gd = RET_QK_DIM, RET_V_DIM, SGU_GROUP_DIM
    half = D_MODEL // 2
    halves = (slice(0, half), slice(half, D_MODEL))
    row_blocks = [slice(c * CHUNK, (c + 1) * CHUNK) for c in range(n_chunks)]
    block = pl.program_id(0)

    @pl.when(block == 0)
    def _():
        mg_s[...] = jnp.zeros_like(mg_s)

    def tile_body(j, carry):
        tile = jnp.minimum(block, n_blocks - 1) * tiles_per_block + j
        tile_in_seq = lax.rem(tile, tiles_per_seq)
        rows_j = pl.ds(pl.multiple_of(j * ts, ts), ts)

        def proj(col0, hf):
            c0 = col0 + hf * half
            return _wdot(h_s[...], win_ref[:, c0:c0 + half])

        def rotary(val, cos, sin, row_factor_ref, plain_s, scaled_s):
            for hd in range(RET_HEADS):
                lanes = slice(hd * dk, (hd + 1) * dk)
                xh = val[:, lanes]
                xr = xh * cos + pltpu.roll(xh, dk // 2, 1) * sin
                plain_s[:, lanes] = xr.astype(_BF16)
                scaled_s[:, lanes] = (xr * row_factor_ref[hd]).astype(_BF16)

        def head_slices(c, hd):
            return row_blocks[c], slice(hd * dk, (hd + 1) * dk), slice(hd * dv, (hd + 1) * dv)

        def mix_unit(c, g):
            rows = row_blocks[c]
            gl = slice(g * gd, (g + 1) * gd)
            mix_s[rows, gl] = _dot(wsc_ref[g], vn_s[rows, gl]) + bs_ref[g]

        units = [(c, hd) for c in range(n_chunks) for hd in range(RET_HEADS)]


        for hf in halves:
            x1_s[:, hf] = xl_ref[0, rows_j, hf] + _wdot(mg_s[j], wo_ref[:, hf])
        h_s[...] = _rms(x_ref[0, rows_j, :], gm_ref[...]).astype(_BF16)
        q_raw = proj(_QK0, 0)
        hp_s[...] = _rms(x1_s[...], gp_ref[...]).astype(_BF16)
        k_raw = proj(_QK0, 1)
        pg = [_wdot(hp_s[...], wpg_ref[:, hf]) for hf in halves]
        rotary(q_raw, cq_ref[rows_j, :], sq_ref[rows_j, :], xi_ref, q_s, qx_s)
        sv0 = proj(_SV0, 0)
        rotary(k_raw, ck_ref[rows_j, :], sk_ref[rows_j, :], zeta_ref, k_s, kz_s)
        v_s[:, halves[0]] = proj(_V0, 0).astype(_BF16)
        g_s[:, halves[0]] = _gelu(sv0)
        sv1 = proj(_SV0, 1)
        v_s[:, halves[1]] = proj(_V0, 1).astype(_BF16)
        pp = _wdot(pl_ref[0, rows_j, :].astype(_BF16), wpp_ref[...])
        x2 = x1_s[...] + jnp.concatenate([jax.nn.sigmoid(g) for g in pg], axis=1) * pp
        o_ref[0, rows_j, :] = _rms(x2, gf_ref[...])
        rg0 = proj(_RG0, 0)

        scores, kv = {}, {}
        for c, hd in units:
            rows, lanes, vl = head_slices(c, hd)
            scores[c, hd] = lax.dot_general(q_s[rows, lanes], k_s[rows, lanes],
                                            (((1,), (1,)), ((), ())),
                                            preferred_element_type=_F32)
        g_s[:, halves[1]] = _gelu(sv1)
        for c, hd in units:
            rows, lanes, vl = head_slices(c, hd)
            kv[c, hd] = lax.dot_general(kz_s[rows, lanes], v_s[rows, vl],
                                        (((0,), (0,)), ((), ())),
                                        preferred_element_type=_F32)
        rg1 = proj(_RG0, 1)
        gate_s[:, halves[0]] = _silu(rg0)
        prev = {}
        for hd in range(RET_HEADS):
            st = jnp.where(tile_in_seq == 0, 0.0, state_ref[hd])
            for c in range(n_chunks):
                prev[c, hd] = st
                st = kv[c, hd] + cdec_ref[hd] * st
            state_ref[hd] = st
        r = {}
        for c, hd in units:
            rows, lanes, vl = head_slices(c, hd)
            lhs = jnp.concatenate([(scores[c, hd] * decay_ref[hd]).astype(_BF16),
                                   qx_s[rows, lanes]], axis=1)
            rhs = jnp.concatenate([v_s[rows, vl], prev[c, hd].astype(_BF16)], axis=0)
            r[c, hd] = _dot(lhs, rhs)
        gate_s[:, halves[1]] = _silu(rg1)

        su0 = proj(_SU0, 0)
        vn_s[...] = _unit(g_s[...], GN_EPS).astype(_BF16)
        su1 = proj(_SU0, 1)
        for c, hd in units:
            rows, lanes, vl = head_slices(c, hd)
            ret_s[rows, vl] = (_unit(r[c, hd], GN_EPS) * gate_s[rows, vl]).astype(_BF16)
        sg0 = proj(_SG0, 0)
        for c in range(n_chunks):
            for g in range(SGU_GROUPS):
                mix_unit(c, g)
        gu_s[:, halves[0]] = _gelu(su0)
        sg1 = proj(_SG0, 1)
        ro0 = _wdot(ret_s[...], wro_ref[:, halves[0]])
        gu_s[:, halves[1]] = _gelu(su1)
        ro1 = _wdot(ret_s[...], wro_ref[:, halves[1]])
        sgu_s[:, halves[0]] = (gu_s[:, halves[0]] * mix_s[:, halves[0]] * _silu(sg0)).astype(_BF16)
        mr0 = proj(_MR0, 0)
        sgu_s[:, halves[1]] = (gu_s[:, halves[1]] * mix_s[:, halves[1]] * _silu(sg1)).astype(_BF16)
        mr1 = proj(_MR0, 1)
        ms0 = proj(_MS0, 0)
        t_s[:, halves[0]] = jax.nn.sigmoid(mr0) * ro0
        ms1 = proj(_MS0, 1)
        t_s[:, halves[1]] = jax.nn.sigmoid(mr1) * ro1
        so0 = _wdot(sgu_s[...], wso_ref[:, halves[0]])
        sm0 = jax.nn.sigmoid(ms0)
        so1 = _wdot(sgu_s[...], wso_ref[:, halves[1]])
        sm1 = jax.nn.sigmoid(ms1)
        mg_s[j, :, halves[0]] = (t_s[:, halves[0]] + sm0 * so0).astype(_BF16)
        mg_s[j, :, halves[1]] = (t_s[:, halves[1]] + sm1 * so1).astype(_BF16)
        return carry

    lax.fori_loop(0, tiles_per_block, tile_body, 0, unroll=True)


def _position_tables(seq, ts):
    dk = RET_QK_DIM
    half = dk // 2
    inv = ROPE_BASE ** (-jnp.arange(half, dtype=_F32) / half)
    ang = jnp.arange(seq, dtype=_F32)[:, None] * inv[None, :]
    cos, sin = jnp.cos(ang), jnp.sin(ang)
    cos_full = jnp.concatenate([cos, cos], axis=-1)
    sin_signed = jnp.concatenate([-sin, sin], axis=-1)
    scale = dk ** -0.5

    log_g = jnp.log(1.0 - 2.0 ** (-5.0 - jnp.arange(RET_HEADS, dtype=_F32)))
    idx = jnp.arange(CHUNK, dtype=_F32)
    diff = idx[:, None] - idx[None, :]
    decay = jnp.where(diff[None] >= 0,
                      jnp.exp(jnp.maximum(diff, 0.0)[None] * log_g[:, None, None]), 0.0)
    zeta = jnp.exp((CHUNK - 1.0 - idx)[None, :] * log_g[:, None])
    xi = jnp.exp((idx + 1.0)[None, :] * log_g[:, None])
    cdec = jnp.exp(CHUNK * log_g)
    reps = ts // CHUNK
    tile = lambda t: jnp.broadcast_to(jnp.tile(t, (1, reps))[:, :, None],
                                      (RET_HEADS, ts, dk))
    return (cos_full * scale, sin_signed * scale, cos_full, sin_signed,
            tile(xi), tile(zeta), decay,
            jnp.broadcast_to(cdec[:, None, None], (RET_HEADS, 1, RET_V_DIM)))


def _const_spec(shape):
    zeros = (0,) * len(shape)
    return pl.BlockSpec(shape, lambda i: zeros, pipeline_mode=pl.Buffered(1))


@jax.jit
def kernel(x, p, w_in, w_ret_out, w_sgu_out, w_out, sgu_ws, sgu_bs, w_ple_gate,
           w_ple_proj, g_mixer, g_ple, g_final):
    batch, seq, d = x.shape
    depth = w_in.shape[0]
    ts = TOKENS_PER_TILE
    bt = ts * TILES_PER_STEP
    assert depth == 1 and d == D_MODEL and seq % bt == 0 and ts % CHUNK == 0
    n_blocks = batch * seq // bt
    cq, sq, ck, sk, xi, zeta, decay, cdec = _position_tables(seq, ts)
    tril = jnp.tril(jnp.ones((CHUNK, CHUNK), _F32))

    x_blocks = x.reshape(n_blocks, bt, d)
    p_blocks = p[0].reshape(n_blocks, bt, PLE_DIM)
    cur_spec = lambda width: pl.BlockSpec(
        (1, bt, width), lambda b: (jnp.minimum(b, n_blocks - 1), 0, 0))
    lag_spec = lambda width: pl.BlockSpec(
        (1, bt, width), lambda b: (jnp.maximum(b - 1, 0), 0, 0))

    ws_causal = (sgu_ws[0] * tril[None]).astype(_BF16)
    bs_rows = jnp.broadcast_to(sgu_bs[0][:, :, None],
                               (SGU_GROUPS, CHUNK, SGU_GROUP_DIM)).astype(_F32)
    (win_p,) = _pack_rows(w_in[0])
    wro_p, wso_p, wo_p, wpg_p, wpp_p = _pack_rows(
        w_ret_out[0], w_sgu_out[0], w_out[0], w_ple_gate[0], w_ple_proj[0])
    blocks_per_seq = seq // bt
    pos_spec = pl.BlockSpec(
        (bt, RET_QK_DIM), lambda b: (lax.rem(jnp.minimum(b, n_blocks - 1), blocks_per_seq), 0))
    consts = (xi, zeta, decay, cdec,
              win_p, wro_p, wso_p, wo_p, ws_causal, bs_rows, wpg_p, wpp_p,
              g_mixer[0][None, :], g_ple[0][None, :], g_final[None, :])
    act = lambda width, dt: pltpu.VMEM((ts, width), dt)
    out = pl.pallas_call(
        functools.partial(_block_kernel, seq // ts, n_blocks),
        grid=(n_blocks + 1,),
        in_specs=[cur_spec(D_MODEL), lag_spec(D_MODEL), lag_spec(PLE_DIM)]
                 + [pos_spec] * 4 + [_const_spec(c.shape) for c in consts],
        out_specs=lag_spec(D_MODEL),
        out_shape=jax.ShapeDtypeStruct(x_blocks.shape, x.dtype),
        scratch_shapes=[
            pltpu.VMEM((RET_HEADS, RET_QK_DIM, RET_V_DIM), _F32),
            act(D_MODEL, _BF16),
            act(RET_HEADS * RET_QK_DIM, _BF16),
            act(RET_HEADS * RET_QK_DIM, _BF16),
            act(RET_HEADS * RET_QK_DIM, _BF16),
            act(RET_HEADS * RET_QK_DIM, _BF16),
            act(D_MODEL, _BF16),
            act(D_MODEL, _F32),
            act(D_MODEL, _BF16),
            act(D_MODEL, _F32),
            act(D_MODEL, _BF16),
            act(D_MODEL, _F32),
            act(D_MODEL, _F32),
            act(D_MODEL, _BF16),
            act(D_MODEL, _F32),
            pltpu.VMEM((TILES_PER_STEP, ts, D_MODEL), _BF16),
            act(D_MODEL, _F32),
            act(D_MODEL, _BF16),
        ],
        compiler_params=pltpu.CompilerParams(
            dimension_semantics=("arbitrary",),
            vmem_limit_bytes=V7X_VMEM_LIMIT_BYTES),
        name="hybrid_block",
    )(x_blocks, x_blocks, p_blocks, cq, sq, ck, sk, *consts)
    return out.reshape(x.shape)
```

```python
import functools

import jax
import jax.numpy as jnp
from jax import lax
from jax.experimental import pallas as pl
from jax.experimental.pallas import tpu as pltpu

D_MODEL = 1024
PLE_DIM = 256
CHUNK = 128
RET_HEADS = 4
RET_V_DIM = D_MODEL // RET_HEADS
RET_QK_DIM = RET_V_DIM // 2
SGU_GROUPS = 4
SGU_GROUP_DIM = D_MODEL // SGU_GROUPS
ROPE_BASE = 10000.0
NORM_EPS = 1e-6
GN_EPS = 1e-5

_QK0 = 0
_V0 = 2 * RET_HEADS * RET_QK_DIM
_RG0 = _V0 + D_MODEL
_SU0 = _RG0 + D_MODEL
_SV0 = _SU0 + D_MODEL
_SG0 = _SV0 + D_MODEL
_MR0 = _SG0 + D_MODEL
_MS0 = _MR0 + D_MODEL
IN_WIDTH = _MS0 + D_MODEL

TOKENS_PER_TILE = 256
TILES_PER_STEP = 2
V7X_VMEM_LIMIT_BYTES = 60 * 1024 * 1024
PACK_BLOCK_ROWS = 512
PACK_BLOCK_COLS = 2048
PACKED_ROW_ALIGN = 16

_F32 = jnp.float32
_BF16 = jnp.bfloat16


def _dot(a, b):
    return jnp.dot(a, b, preferred_element_type=_F32)


def _pack_kernel(*refs):
    n = len(refs) // 2
    for w_ref, o_ref in zip(refs[:n], refs[n:]):
        o_ref[...] = pltpu.bitcast(w_ref[...].astype(_BF16), jnp.uint32)


def _pack_rows(*ws):
    n = ws[0].shape[1]
    nb = min(n, PACK_BLOCK_COLS)
    slabs = max(w.shape[0] for w in ws) // PACK_BLOCK_ROWS
    assert n % nb == 0
    assert all(w.shape[1] == n and w.shape[0] % (PACKED_ROW_ALIGN * slabs) == 0 for w in ws)
    kbs = [w.shape[0] // slabs for w in ws]
    return pl.pallas_call(
        _pack_kernel,
        grid=(slabs, n // nb),
        in_specs=[pl.BlockSpec((kb, nb), lambda i, j: (i, j)) for kb in kbs],
        out_specs=[pl.BlockSpec((kb // 2, nb), lambda i, j: (i, j)) for kb in kbs],
        out_shape=[jax.ShapeDtypeStruct((w.shape[0] // 2, n), jnp.uint32) for w in ws],
        name="pack_weight_rows",
    )(*ws)


def _wdot(a, w_packed):
    return _dot(a, pltpu.bitcast(w_packed, _BF16))


def _rms(x, g):
    ms = jnp.mean(x * x, axis=-1, keepdims=True)
    return x * lax.rsqrt(ms + NORM_EPS) * g


def _unit(x, eps):
    mu = jnp.mean(x, axis=-1, keepdims=True)
    d = x - mu
    var = jnp.mean(d * d, axis=-1, keepdims=True)
    return d * lax.rsqrt(var + eps)


def _silu(x):
    return x * jax.nn.sigmoid(x)


def _gelu(x):
    return 0.5 * x * (1.0 + lax.erf(x * (0.5 ** 0.5)))


def _block_kernel(tiles_per_seq, n_blocks,
                  x_ref, xl_ref, pl_ref, cq_ref, sq_ref, ck_ref, sk_ref, xi_ref, zeta_ref,
                  decay_ref, cdec_ref, win_ref, wro_ref, wso_ref, wo_ref, wsc_ref,
                  bs_ref, wpg_ref, wpp_ref, gm_ref, gp_ref, gf_ref,
                  o_ref,
                  state_ref, h_s, q_s, qx_s, k_s, kz_s, v_s, gate_s, ret_s, g_s, vn_s,
                  gu_s, mix_s, sgu_s, t_s, mg_s, x1_s, hp_s):
    ts = TOKENS_PER_TILE
    tiles_per_block = x_ref.shape[1] // ts
    n_chunks = ts // CHUNK
    dk, dv, gd = RET_QK_DIM, RET_V_DIM, SGU_GROUP_DIM
    half = D_MODEL // 2
    halves = (slice(0, half), slice(half, D_MODEL))
    row_blocks = [slice(c * CHUNK, (c + 1) * CHUNK) for c in range(n_chunks)]
    block = pl.program_id(0)

    @pl.when(block == 0)
    def _():
        mg_s[...] = jnp.zeros_like(mg_s)

    def tile_body(j, carry):
        tile = jnp.minimum(block, n_blocks - 1) * tiles_per_block + j
        tile_in_seq = lax.rem(tile, tiles_per_seq)
        rows_j = pl.ds(pl.multiple_of(j * ts, ts), ts)

        def proj(col0, hf):
            c0 = col0 + hf * half
            return _wdot(h_s[...], win_ref[:, c0:c0 + half])

        def rotary(val, cos, sin, row_factor_ref, plain_s, scaled_s):
            for hd in range(RET_HEADS):
                lanes = slice(hd * dk, (hd + 1) * dk)
                xh = val[:, lanes]
                xr = xh * cos + pltpu.roll(xh, dk // 2, 1) * sin
                plain_s[:, lanes] = xr.astype(_BF16)
                scaled_s[:, lanes] = (xr * row_factor_ref[hd]).astype(_BF16)

        def head_slices(c, hd):
            return row_blocks[c], slice(hd * dk, (hd + 1) * dk), slice(hd * dv, (hd + 1) * dv)

        def mix_unit(c, g):
            rows = row_blocks[c]
            gl = slice(g * gd, (g + 1) * gd)
            mix_s[rows, gl] = _dot(wsc_ref[g], vn_s[rows, gl]) + bs_ref[g]

        units = [(c, hd) for c in range(n_chunks) for hd in range(RET_HEADS)]


        for hf in halves:
            x1_s[:, hf] = xl_ref[0, rows_j, hf] + _wdot(mg_s[j], wo_ref[:, hf])
        h_s[...] = _rms(x_ref[0, rows_j, :], gm_ref[...]).astype(_BF16)
        q_raw = proj(_QK0, 0)
        hp_s[...] = _rms(x1_s[...], gp_ref[...]).astype(_BF16)
        k_raw = proj(_QK0, 1)
        pg = [_wdot(hp_s[...], wpg_ref[:, hf]) for hf in halves]
        rotary(q_raw, cq_ref[rows_j, :], sq_ref[rows_j, :], xi_ref, q_s, qx_s)
        sv0 = proj(_SV0, 0)
        rotary(k_raw, ck_ref[rows_j, :], sk_ref[rows_j, :], zeta_ref, k_s, kz_s)
        v_s[:, halves[0]] = proj(_V0, 0).astype(_BF16)
        g_s[:, halves[0]] = _gelu(sv0)
        sv1 = proj(_SV0, 1)
        v_s[:, halves[1]] = proj(_V0, 1).astype(_BF16)
        pp = _wdot(pl_ref[0, rows_j, :].astype(_BF16), wpp_ref[...])
        x2 = x1_s[...] + jnp.concatenate([jax.nn.sigmoid(g) for g in pg], axis=1) * pp
        o_ref[0, rows_j, :] = _rms(x2, gf_ref[...])
        rg0 = proj(_RG0, 0)

        scores, kv = {}, {}
        for c, hd in units:
            rows, lanes, vl = head_slices(c, hd)
            scores[c, hd] = lax.dot_general(q_s[rows, lanes], k_s[rows, lanes],
                                            (((1,), (1,)), ((), ())),
                                            preferred_element_type=_F32)
        g_s[:, halves[1]] = _gelu(sv1)
        for c, hd in units:
            rows, lanes, vl = head_slices(c, hd)
            kv[c, hd] = lax.dot_general(kz_s[rows, lanes], v_s[rows, vl],
                                        (((0,), (0,)), ((), ())),
                                        preferred_element_type=_F32)
        rg1 = proj(_RG0, 1)
        gate_s[:, halves[0]] = _silu(rg0)
        prev = {}
        for hd in range(RET_HEADS):
            st = jnp.where(tile_in_seq == 0, 0.0, state_ref[hd])
            for c in range(n_chunks):
                prev[c, hd] = st
                st = kv[c, hd] + cdec_ref[hd] * st
            state_ref[hd] = st
        r = {}
        for c, hd in units:
            rows, lanes, vl = head_slices(c, hd)
            lhs = jnp.concatenate([(scores[c, hd] * decay_ref[hd]).astype(_BF16),
                                   qx_s[rows, lanes]], axis=1)
            rhs = jnp.concatenate([v_s[rows, vl], prev[c, hd].astype(_BF16)], axis=0)
            r[c, hd] = _dot(lhs, rhs)
        gate_s[:, halves[1]] = _silu(rg1)

        su0 = proj(_SU0, 0)
        vn_s[...] = _unit(g_s[...], GN_EPS).astype(_BF16)
        su1 = proj(_SU0, 1)
        for c, hd in units:
            rows, lanes, vl = head_slices(c, hd)
            ret_s[rows, vl] = (_unit(r[c, hd], GN_EPS) * gate_s[rows, vl]).astype(_BF16)
        sg0 = proj(_SG0, 0)
        for c in range(n_chunks):
            for g in range(SGU_GROUPS):
                mix_unit(c, g)
        gu_s[:, halves[0]] = _gelu(su0)
        sg1 = proj(_SG0, 1)
        ro0 = _wdot(ret_s[...], wro_ref[:, halves[0]])
        gu_s[:, halves[1]] = _gelu(su1)
        ro1 = _wdot(ret_s[...], wro_ref[:, halves[1]])
        sgu_s[:, halves[0]] = (gu_s[:, halves[0]] * mix_s[:, halves[0]] * _silu(sg0)).astype(_BF16)
        mr0 = proj(_MR0, 0)
        sgu_s[:, halves[1]] = (gu_s[:, halves[1]] * mix_s[:, halves[1]] * _silu(sg1)).astype(_BF16)
        mr1 = proj(_MR0, 1)
        ms0 = proj(_MS0, 0)
        t_s[:, halves[0]] = jax.nn.sigmoid(mr0) * ro0
        ms1 = proj(_MS0, 1)
        t_s[:, halves[1]] = jax.nn.sigmoid(mr1) * ro1
        so0 = _wdot(sgu_s[...], wso_ref[:, halves[0]])
        sm0 = jax.nn.sigmoid(ms0)
        so1 = _wdot(sgu_s[...], wso_ref[:, halves[1]])
        sm1 = jax.nn.sigmoid(ms1)
        mg_s[j, :, halves[0]] = (t_s[:, halves[0]] + sm0 * so0).astype(_BF16)
        mg_s[j, :, halves[1]] = (t_s[:, halves[1]] + sm1 * so1).astype(_BF16)
        return carry

    lax.fori_loop(0, tiles_per_block, tile_body, 0, unroll=True)


def _position_tables(seq, ts):
    dk = RET_QK_DIM
    half = dk // 2
    inv = ROPE_BASE ** (-jnp.arange(half, dtype=_F32) / half)
    ang = jnp.arange(seq, dtype=_F32)[:, None] * inv[None, :]
    cos, sin = jnp.cos(ang), jnp.sin(ang)
    cos_full = jnp.concatenate([cos, cos], axis=-1)
    sin_signed = jnp.concatenate([-sin, sin], axis=-1)
    scale = dk ** -0.5

    log_g = jnp.log(1.0 - 2.0 ** (-5.0 - jnp.arange(RET_HEADS, dtype=_F32)))
    idx = jnp.arange(CHUNK, dtype=_F32)
    diff = idx[:, None] - idx[None, :]
    decay = jnp.where(diff[None] >= 0,
                      jnp.exp(jnp.maximum(diff, 0.0)[None] * log_g[:, None, None]), 0.0)
    zeta = jnp.exp((CHUNK - 1.0 - idx)[None, :] * log_g[:, None])
    xi = jnp.exp((idx + 1.0)[None, :] * log_g[:, None])
    cdec = jnp.exp(CHUNK * log_g)
    reps = ts // CHUNK
    tile = lambda t: jnp.broadcast_to(jnp.tile(t, (1, reps))[:, :, None],
                                      (RET_HEADS, ts, dk))
    return (cos_full * scale, sin_signed * scale, cos_full, sin_signed,
            tile(xi), tile(zeta), decay,
            jnp.broadcast_to(cdec[:, None, None], (RET_HEADS, 1, RET_V_DIM)))


def _const_spec(shape):
    zeros = (0,) * len(shape)
    return pl.BlockSpec(shape, lambda i: zeros, pipeline_mode=pl.Buffered(1))


@jax.jit
def kernel(x, p, w_in, w_ret_out, w_sgu_out, w_out, sgu_ws, sgu_bs, w_ple_gate,
           w_ple_proj, g_mixer, g_ple, g_final):
    batch, seq, d = x.shape
    depth = w_in.shape[0]
    ts = TOKENS_PER_TILE
    bt = ts * TILES_PER_STEP
    assert depth == 1 and d == D_MODEL and seq % bt == 0 and ts % CHUNK == 0
    n_blocks = batch * seq // bt
    cq, sq, ck, sk, xi, zeta, decay, cdec = _position_tables(seq, ts)
    tril = jnp.tril(jnp.ones((CHUNK, CHUNK), _F32))

    x_blocks = x.reshape(n_blocks, bt, d)
    p_blocks = p[0].reshape(n_blocks, bt, PLE_DIM)
    cur_spec = lambda width: pl.BlockSpec(
        (1, bt, width), lambda b: (jnp.minimum(b, n_blocks - 1), 0, 0))
    lag_spec = lambda width: pl.BlockSpec(
        (1, bt, width), lambda b: (jnp.maximum(b - 1, 0), 0, 0))

    ws_causal = (sgu_ws[0] * tril[None]).astype(_BF16)
    bs_rows = jnp.broadcast_to(sgu_bs[0][:, :, None],
                               (SGU_GROUPS, CHUNK, SGU_GROUP_DIM)).astype(_F32)
    (win_p,) = _pack_rows(w_in[0])
    wro_p, wso_p, wo_p, wpg_p, wpp_p = _pack_rows(
        w_ret_out[0], w_sgu_out[0], w_out[0], w_ple_gate[0], w_ple_proj[0])
    blocks_per_seq = seq // bt
    pos_spec = pl.BlockSpec(
        (bt, RET_QK_DIM), lambda b: (lax.rem(jnp.minimum(b, n_blocks - 1), blocks_per_seq), 0))
    consts = (xi, zeta, decay, cdec,
              win_p, wro_p, wso_p, wo_p, ws_causal, bs_rows, wpg_p, wpp_p,
              g_mixer[0][None, :], g_ple[0][None, :], g_final[None, :])
    act = lambda width, dt: pltpu.VMEM((ts, width), dt)
    out = pl.pallas_call(
        functools.partial(_block_kernel, seq // ts, n_blocks),
        grid=(n_blocks + 1,),
        in_specs=[cur_spec(D_MODEL), lag_spec(D_MODEL), lag_spec(PLE_DIM)]
                 + [pos_spec] * 4 + [_const_spec(c.shape) for c in consts],
        out_specs=lag_spec(D_MODEL),
        out_shape=jax.ShapeDtypeStruct(x_blocks.shape, x.dtype),
        scratch_shapes=[
            pltpu.VMEM((RET_HEADS, RET_QK_DIM, RET_V_DIM), _F32),
            act(D_MODEL, _BF16),
            act(RET_HEADS * RET_QK_DIM, _BF16),
            act(RET_HEADS * RET_QK_DIM, _BF16),
            act(RET_HEADS * RET_QK_DIM, _BF16),
            act(RET_HEADS * RET_QK_DIM, _BF16),
            act(D_MODEL, _BF16),
            act(D_MODEL, _F32),
            act(D_MODEL, _BF16),
            act(D_MODEL, _F32),
            act(D_MODEL, _BF16),
            act(D_MODEL, _F32),
            act(D_MODEL, _F32),
            act(D_MODEL, _BF16),
            act(D_MODEL, _F32),
            pltpu.VMEM((TILES_PER_STEP, ts, D_MODEL), _BF16),
            act(D_MODEL, _F32),
            act(D_MODEL, _BF16),
        ],
        compiler_params=pltpu.CompilerParams(
            dimension_semantics=("arbitrary",),
            vmem_limit_bytes=V7X_VMEM_LIMIT_BYTES),
        name="hybrid_block",
    )(x_blocks, x_blocks, p_blocks, cq, sq, ck, sk, *consts)
    return out.reshape(x.shape)
```

```python
import functools

import jax
import jax.numpy as jnp
from jax import lax
from jax.experimental import pallas as pl
from jax.experimental.pallas import tpu as pltpu

D_MODEL = 1024
PLE_DIM = 256
CHUNK = 128
RET_HEADS = 4
RET_V_DIM = D_MODEL // RET_HEADS
RET_QK_DIM = RET_V_DIM // 2
SGU_GROUPS = 4
SGU_GROUP_DIM = D_MODEL // SGU_GROUPS
ROPE_BASE = 10000.0
NORM_EPS = 1e-6
GN_EPS = 1e-5

_QK0 = 0
_V0 = 2 * RET_HEADS * RET_QK_DIM
_RG0 = _V0 + D_MODEL
_SU0 = _RG0 + D_MODEL
_SV0 = _SU0 + D_MODEL
_SG0 = _SV0 + D_MODEL
_MR0 = _SG0 + D_MODEL
_MS0 = _MR0 + D_MODEL
IN_WIDTH = _MS0 + D_MODEL

TOKENS_PER_TILE = 256
TILES_PER_STEP = 2
V7X_VMEM_LIMIT_BYTES = 63 * 1024 * 1024
PACK_BLOCK_ROWS = 512
PACK_BLOCK_COLS = 2048
PACKED_ROW_ALIGN = 16

_F32 = jnp.float32
_BF16 = jnp.bfloat16


def _dot(a, b):
    return jnp.dot(a, b, preferred_element_type=_F32)


def _pack_kernel(*refs):
    n = len(refs) // 2
    for w_ref, o_ref in zip(refs[:n], refs[n:]):
        o_ref[...] = pltpu.bitcast(w_ref[...].astype(_BF16), jnp.uint32)


def _pack_rows(*ws):
    n = ws[0].shape[1]
    nb = min(n, PACK_BLOCK_COLS)
    slabs = max(w.shape[0] for w in ws) // PACK_BLOCK_ROWS
    assert n % nb == 0
    assert all(w.shape[1] == n and w.shape[0] % (PACKED_ROW_ALIGN * slabs) == 0 for w in ws)
    kbs = [w.shape[0] // slabs for w in ws]
    return pl.pallas_call(
        _pack_kernel,
        grid=(slabs, n // nb),
        in_specs=[pl.BlockSpec((kb, nb), lambda i, j: (i, j)) for kb in kbs],
        out_specs=[pl.BlockSpec((kb // 2, nb), lambda i, j: (i, j)) for kb in kbs],
        out_shape=[jax.ShapeDtypeStruct((w.shape[0] // 2, n), jnp.uint32) for w in ws],
        name="pack_weight_rows",
    )(*ws)


def _wdot(a, w_packed):
    return _dot(a, pltpu.bitcast(w_packed, _BF16))


def _rms(x, g):
    ms = jnp.mean(x * x, axis=-1, keepdims=True)
    return x * lax.rsqrt(ms + NORM_EPS) * g


def _unit(x, eps):
    mu = jnp.mean(x, axis=-1, keepdims=True)
    d = x - mu
    var = jnp.mean(d * d, axis=-1, keepdims=True)
    return d * lax.rsqrt(var + eps)


def _silu(x):
    return x * jax.nn.sigmoid(x)


def _gelu(x):
    return 0.5 * x * (1.0 + lax.erf(x * (0.5 ** 0.5)))


def _block_kernel(tiles_per_seq, n_blocks,
                  x_ref, xl_ref, pl_ref, cq_ref, sq_ref, ck_ref, sk_ref, xi_ref, zeta_ref,
                  decay_ref, cdec_ref, win_ref, wro_ref, wso_ref, wo_ref, wsc_ref,
                  bs_ref, wpg_ref, wpp_ref, gm_ref, gp_ref, gf_ref,
                  o_ref,
                  state_ref, h_s, q_s, qx_s, k_s, kz_s, v_s, gate_s, ret_s, g_s, vn_s,
                  gu_s, mix_s, sgu_s, t_s, mg_s, x1_s, hp_s):
    ts = TOKENS_PER_TILE
    tiles_per_block = x_ref.shape[1] // ts
    n_chunks = ts // CHUNK
    dk, dv, gd = RET_QK_DIM, RET_V_DIM, SGU_GROUP_DIM
    half = D_MODEL // 2
    halves = (slice(0, half), slice(half, D_MODEL))
    row_blocks = [slice(c * CHUNK, (c + 1) * CHUNK) for c in range(n_chunks)]
    block = pl.program_id(0)

    @pl.when(block == 0)
    def _():
        mg_s[...] = jnp.zeros_like(mg_s)

    def tile_body(j, carry):
        tile = jnp.minimum(block, n_blocks - 1) * tiles_per_block + j
        tile_in_seq = lax.rem(tile, tiles_per_seq)
        rows_j = pl.ds(pl.multiple_of(j * ts, ts), ts)
        pos_rows = pl.ds(pl.multiple_of(tile_in_seq * ts, ts), ts)

        def proj(col0, hf):
            c0 = col0 + hf * half
            return _wdot(h_s[...], win_ref[:, c0:c0 + half])

        def rotary(val, cos, sin, row_factor_ref, plain_s, scaled_s):
            for hd in range(RET_HEADS):
                lanes = slice(hd * dk, (hd + 1) * dk)
                xh = val[:, lanes]
                xr = xh * cos + pltpu.roll(xh, dk // 2, 1) * sin
                plain_s[:, lanes] = xr.astype(_BF16)
                scaled_s[:, lanes] = (xr * row_factor_ref[hd]).astype(_BF16)

        def head_slices(c, hd):
            return row_blocks[c], slice(hd * dk, (hd + 1) * dk), slice(hd * dv, (hd + 1) * dv)

        def mix_unit(c, g):
            rows = row_blocks[c]
            gl = slice(g * gd, (g + 1) * gd)
            mix_s[rows, gl] = _dot(wsc_ref[g], vn_s[rows, gl]) + bs_ref[g]

        units = [(c, hd) for c in range(n_chunks) for hd in range(RET_HEADS)]


        for hf in halves:
            x1_s[:, hf] = xl_ref[0, rows_j, hf] + _wdot(mg_s[j], wo_ref[:, hf])
        h_s[...] = _rms(x_ref[0, rows_j, :], gm_ref[...]).astype(_BF16)
        q_raw = proj(_QK0, 0)
        hp_s[...] = _rms(x1_s[...], gp_ref[...]).astype(_BF16)
        k_raw = proj(_QK0, 1)
        pg = [_wdot(hp_s[...], wpg_ref[:, hf]) for hf in halves]
        rotary(q_raw, cq_ref[pos_rows, :], sq_ref[pos_rows, :], xi_ref, q_s, qx_s)
        sv0 = proj(_SV0, 0)
        rotary(k_raw, ck_ref[pos_rows, :], sk_ref[pos_rows, :], zeta_ref, k_s, kz_s)
        v_s[:, halves[0]] = proj(_V0, 0).astype(_BF16)
        g_s[:, halves[0]] = _gelu(sv0)
        sv1 = proj(_SV0, 1)
        v_s[:, halves[1]] = proj(_V0, 1).astype(_BF16)
        pp = _wdot(pl_ref[0, rows_j, :].astype(_BF16), wpp_ref[...])
        x2 = x1_s[...] + jnp.concatenate([jax.nn.sigmoid(g) for g in pg], axis=1) * pp
        o_ref[0, rows_j, :] = _rms(x2, gf_ref[...])
        rg0 = proj(_RG0, 0)

        scores, kv = {}, {}
        for c, hd in units:
            rows, lanes, vl = head_slices(c, hd)
            scores[c, hd] = lax.dot_general(q_s[rows, lanes], k_s[rows, lanes],
                                            (((1,), (1,)), ((), ())),
                                            preferred_element_type=_F32)
        g_s[:, halves[1]] = _gelu(sv1)
        for c, hd in units:
            rows, lanes, vl = head_slices(c, hd)
            kv[c, hd] = lax.dot_general(kz_s[rows, lanes], v_s[rows, vl],
                                        (((0,), (0,)), ((), ())),
                                        preferred_element_type=_F32)
        rg1 = proj(_RG0, 1)
        gate_s[:, halves[0]] = _silu(rg0)
        prev = {}
        for hd in range(RET_HEADS):
            st = jnp.where(tile_in_seq == 0, 0.0, state_ref[hd])
            for c in range(n_chunks):
                prev[c, hd] = st
                st = kv[c, hd] + cdec_ref[hd] * st
            state_ref[hd] = st
        r = {}
        for c, hd in units:
            rows, lanes, vl = head_slices(c, hd)
            lhs = jnp.concatenate([(scores[c, hd] * decay_ref[hd]).astype(_BF16),
                                   qx_s[rows, lanes]], axis=1)
            rhs = jnp.concatenate([v_s[rows, vl], prev[c, hd].astype(_BF16)], axis=0)
            r[c, hd] = _dot(lhs, rhs)
        gate_s[:, halves[1]] = _silu(rg1)

        su0 = proj(_SU0, 0)
        vn_s[...] = _unit(g_s[...], GN_EPS).astype(_BF16)
        su1 = proj(_SU0, 1)
        for c, hd in units:
            rows, lanes, vl = head_slices(c, hd)
            ret_s[rows, vl] = (_unit(r[c, hd], GN_EPS) * gate_s[rows, vl]).astype(_BF16)
        sg0 = proj(_SG0, 0)
        for c in range(n_chunks):
            for g in range(SGU_GROUPS):
                mix_unit(c, g)
        gu_s[:, halves[0]] = _gelu(su0)
        sg1 = proj(_SG0, 1)
        ro0 = _wdot(ret_s[...], wro_ref[:, halves[0]])
        gu_s[:, halves[1]] = _gelu(su1)
        ro1 = _wdot(ret_s[...], wro_ref[:, halves[1]])
        sgu_s[:, halves[0]] = (gu_s[:, halves[0]] * mix_s[:, halves[0]] * _silu(sg0)).astype(_BF16)
        mr0 = proj(_MR0, 0)
        sgu_s[:, halves[1]] = (gu_s[:, halves[1]] * mix_s[:, halves[1]] * _silu(sg1)).astype(_BF16)
        mr1 = proj(_MR0, 1)
        ms0 = proj(_MS0, 0)
        t_s[:, halves[0]] = jax.nn.sigmoid(mr0) * ro0
        ms1 = proj(_MS0, 1)
        t_s[:, halves[1]] = jax.nn.sigmoid(mr1) * ro1
        so0 = _wdot(sgu_s[...], wso_ref[:, halves[0]])
        sm0 = jax.nn.sigmoid(ms0)
        so1 = _wdot(sgu_s[...], wso_ref[:, halves[1]])
        sm1 = jax.nn.sigmoid(ms1)
        mg_s[j, :, halves[0]] = (t_s[:, halves[0]] + sm0 * so0).astype(_BF16)
        mg_s[j, :, halves[1]] = (t_s[:, halves[1]] + sm1 * so1).astype(_BF16)
        return carry

    lax.fori_loop(0, tiles_per_block, tile_body, 0, unroll=True)


def _position_tables(seq, ts):
    dk = RET_QK_DIM
    half = dk // 2
    inv = ROPE_BASE ** (-jnp.arange(half, dtype=_F32) / half)
    ang = jnp.arange(seq, dtype=_F32)[:, None] * inv[None, :]
    cos, sin = jnp.cos(ang), jnp.sin(ang)
    cos_full = jnp.concatenate([cos, cos], axis=-1)
    sin_signed = jnp.concatenate([-sin, sin], axis=-1)
    scale = dk ** -0.5

    log_g = jnp.log(1.0 - 2.0 ** (-5.0 - jnp.arange(RET_HEADS, dtype=_F32)))
    idx = jnp.arange(CHUNK, dtype=_F32)
    diff = idx[:, None] - idx[None, :]
    decay = jnp.where(diff[None] >= 0,
                      jnp.exp(jnp.maximum(diff, 0.0)[None] * log_g[:, None, None]), 0.0)
    zeta = jnp.exp((CHUNK - 1.0 - idx)[None, :] * log_g[:, None])
    xi = jnp.exp((idx + 1.0)[None, :] * log_g[:, None])
    cdec = jnp.exp(CHUNK * log_g)
    reps = ts // CHUNK
    tile = lambda t: jnp.broadcast_to(jnp.tile(t, (1, reps))[:, :, None],
                                      (RET_HEADS, ts, dk))
    return (cos_full * scale, sin_signed * scale, cos_full, sin_signed,
            tile(xi), tile(zeta), decay,
            jnp.broadcast_to(cdec[:, None, None], (RET_HEADS, 1, RET_V_DIM)))


def _const_spec(shape):
    zeros = (0,) * len(shape)
    return pl.BlockSpec(shape, lambda i: zeros, pipeline_mode=pl.Buffered(1))


@jax.jit
def kernel(x, p, w_in, w_ret_out, w_sgu_out, w_out, sgu_ws, sgu_bs, w_ple_gate,
           w_ple_proj, g_mixer, g_ple, g_final):
    batch, seq, d = x.shape
    depth = w_in.shape[0]
    ts = TOKENS_PER_TILE
    bt = ts * TILES_PER_STEP
    assert depth == 1 and d == D_MODEL and seq % bt == 0 and ts % CHUNK == 0
    n_blocks = batch * seq // bt
    cq, sq, ck, sk, xi, zeta, decay, cdec = _position_tables(seq, ts)
    tril = jnp.tril(jnp.ones((CHUNK, CHUNK), _F32))

    x_blocks = x.reshape(n_blocks, bt, d)
    p_blocks = p[0].reshape(n_blocks, bt, PLE_DIM)
    cur_spec = lambda width: pl.BlockSpec(
        (1, bt, width), lambda b: (jnp.minimum(b, n_blocks - 1), 0, 0))
    lag_spec = lambda width: pl.BlockSpec(
        (1, bt, width), lambda b: (jnp.maximum(b - 1, 0), 0, 0))

    ws_causal = (sgu_ws[0] * tril[None]).astype(_BF16)
    bs_rows = jnp.broadcast_to(sgu_bs[0][:, :, None],
                               (SGU_GROUPS, CHUNK, SGU_GROUP_DIM)).astype(_F32)
    (win_p,) = _pack_rows(w_in[0])
    wro_p, wso_p, wo_p, wpg_p, wpp_p = _pack_rows(
        w_ret_out[0], w_sgu_out[0], w_out[0], w_ple_gate[0], w_ple_proj[0])
    consts = (cq, sq, ck, sk, xi, zeta, decay, cdec,
              win_p, wro_p, wso_p, wo_p, ws_causal, bs_rows, wpg_p, wpp_p,
              g_mixer[0][None, :], g_ple[0][None, :], g_final[None, :])
    act = lambda width, dt: pltpu.VMEM((ts, width), dt)
    out = pl.pallas_call(
        functools.partial(_block_kernel, seq // ts, n_blocks),
        grid=(n_blocks + 1,),
        in_specs=[cur_spec(D_MODEL), lag_spec(D_MODEL), lag_spec(PLE_DIM)]
                 + [_const_spec(c.shape) for c in consts],
        out_specs=lag_spec(D_MODEL),
        out_shape=jax.ShapeDtypeStruct(x_blocks.shape, x.dtype),
        scratch_shapes=[
            pltpu.VMEM((RET_HEADS, RET_QK_DIM, RET_V_DIM), _F32),
            act(D_MODEL, _BF16),
            act(RET_HEADS * RET_QK_DIM, _BF16),
            act(RET_HEADS * RET_QK_DIM, _BF16),
            act(RET_HEADS * RET_QK_DIM, _BF16),
            act(RET_HEADS * RET_QK_DIM, _BF16),
            act(D_MODEL, _BF16),
            act(D_MODEL, _F32),
            act(D_MODEL, _BF16),
            act(D_MODEL, _F32),
            act(D_MODEL, _BF16),
            act(D_MODEL, _F32),
            act(D_MODEL, _F32),
            act(D_MODEL, _BF16),
            act(D_MODEL, _F32),
            pltpu.VMEM((TILES_PER_STEP, ts, D_MODEL), _BF16),
            act(D_MODEL, _F32),
            act(D_MODEL, _BF16),
        ],
        compiler_params=pltpu.CompilerParams(
            dimension_semantics=("arbitrary",),
            vmem_limit_bytes=V7X_VMEM_LIMIT_BYTES),
        name="hybrid_block",
    )(x_blocks, x_blocks, p_blocks, *consts)
    return out.reshape(x.shape)
```

```python
import functools

import jax
import jax.numpy as jnp
from jax import lax
from jax.experimental import pallas as pl
from jax.experimental.pallas import tpu as pltpu

D_MODEL = 1024
PLE_DIM = 256
CHUNK = 128
RET_HEADS = 4
RET_V_DIM = D_MODEL // RET_HEADS
RET_QK_DIM = RET_V_DIM // 2
SGU_GROUPS = 4
SGU_GROUP_DIM = D_MODEL // SGU_GROUPS
ROPE_BASE = 10000.0
NORM_EPS = 1e-6
GN_EPS = 1e-5

_QK0 = 0
_V0 = 2 * RET_HEADS * RET_QK_DIM
_RG0 = _V0 + D_MODEL
_SU0 = _RG0 + D_MODEL
_SV0 = _SU0 + D_MODEL
_SG0 = _SV0 + D_MODEL
_MR0 = _SG0 + D_MODEL
_MS0 = _MR0 + D_MODEL
IN_WIDTH = _MS0 + D_MODEL

TOKENS_PER_TILE = 256
TILES_PER_STEP = 2
V7X_VMEM_LIMIT_BYTES = 60 * 1024 * 1024
PACK_BLOCK_ROWS = 512
PACK_BLOCK_COLS = 2048
PACKED_ROW_ALIGN = 16

_F32 = jnp.float32
_BF16 = jnp.bfloat16


def _dot(a, b):
    return jnp.dot(a, b, preferred_element_type=_F32)


def _pack_kernel(*refs):
    n = len(refs) // 2
    for w_ref, o_ref in zip(refs[:n], refs[n:]):
        o_ref[...] = pltpu.bitcast(w_ref[...].astype(_BF16), jnp.uint32)


def _pack_rows(*ws):
    n = ws[0].shape[1]
    nb = min(n, PACK_BLOCK_COLS)
    slabs = max(w.shape[0] for w in ws) // PACK_BLOCK_ROWS
    assert n % nb == 0
    assert all(w.shape[1] == n and w.shape[0] % (PACKED_ROW_ALIGN * slabs) == 0 for w in ws)
    kbs = [w.shape[0] // slabs for w in ws]
    return pl.pallas_call(
        _pack_kernel,
        grid=(slabs, n // nb),
        in_specs=[pl.BlockSpec((kb, nb), lambda i, j: (i, j)) for kb in kbs],
        out_specs=[pl.BlockSpec((kb // 2, nb), lambda i, j: (i, j)) for kb in kbs],
        out_shape=[jax.ShapeDtypeStruct((w.shape[0] // 2, n), jnp.uint32) for w in ws],
        name="pack_weight_rows",
    )(*ws)


def _wdot(a, w_packed):
    return _dot(a, pltpu.bitcast(w_packed, _BF16))


def _rms(x, g):
    ms = jnp.mean(x * x, axis=-1, keepdims=True)
    return x * lax.rsqrt(ms + NORM_EPS) * g


def _unit(x, eps):
    mu = jnp.mean(x, axis=-1, keepdims=True)
    d = x - mu
    var = jnp.mean(d * d, axis=-1, keepdims=True)
    return d * lax.rsqrt(var + eps)


def _silu(x):
    return x * jax.nn.sigmoid(x)


def _gelu(x):
    return 0.5 * x * (1.0 + lax.erf(x * (0.5 ** 0.5)))


def _block_kernel(tiles_per_seq, n_blocks,
                  x_ref, xl_ref, pl_ref, cq_ref, sq_ref, ck_ref, sk_ref, xi_ref, zeta_ref,
                  decay_ref, cdec_ref, win_ref, wro_ref, wso_ref, wo_ref, wsc_ref,
                  bs_ref, wpg_ref, wpp_ref, gm_ref, gp_ref, gf_ref,
                  o_ref,
                  state_ref, h_s, q_s, qx_s, k_s, kz_s, v_s, gate_s, ret_s, g_s, vn_s,
                  gu_s, mix_s, sgu_s, t_s, mg_s, x1_s, hp_s):
    ts = TOKENS_PER_TILE
    tiles_per_block = x_ref.shape[1] // ts
    n_chunks = ts // CHUNK
    dk, dv, gd = RET_QK_DIM, RET_V_DIM, SGU_GROUP_DIM
    half = D_MODEL // 2
    halves = (slice(0, half), slice(half, D_MODEL))
    row_blocks = [slice(c * CHUNK, (c + 1) * CHUNK) for c in range(n_chunks)]
    block = pl.program_id(0)

    @pl.when(block == 0)
    def _():
        mg_s[...] = jnp.zeros_like(mg_s)

    def out_only(j):
        rows_j = pl.ds(j * ts, ts)
        for hf in halves:
            x1_s[:, hf] = xl_ref[0, rows_j, hf] + _wdot(mg_s[j], wo_ref[:, hf])
        hp = _rms(x1_s[...], gp_ref[...]).astype(_BF16)
        pg = [_wdot(hp, wpg_ref[:, hf]) for hf in halves]
        pp = _wdot(pl_ref[0, rows_j, :].astype(_BF16), wpp_ref[...])
        x2 = x1_s[...] + jnp.concatenate([jax.nn.sigmoid(g) for g in pg], axis=1) * pp
        o_ref[0, rows_j, :] = _rms(x2, gf_ref[...])

    def tile_body(j, carry):
        tile = block * tiles_per_block + j
        tile_in_seq = lax.rem(tile, tiles_per_seq)
        rows_j = pl.ds(pl.multiple_of(j * ts, ts), ts)

        def proj(col0, hf):
            c0 = col0 + hf * half
            return _wdot(h_s[...], win_ref[:, c0:c0 + half])

        def rotary(val, cos, sin, row_factor_ref, plain_s, scaled_s):
            for hd in range(RET_HEADS):
                lanes = slice(hd * dk, (hd + 1) * dk)
                xh = val[:, lanes]
                xr = xh * cos + pltpu.roll(xh, dk // 2, 1) * sin
                plain_s[:, lanes] = xr.astype(_BF16)
                scaled_s[:, lanes] = (xr * row_factor_ref[hd]).astype(_BF16)

        def head_slices(c, hd):
            return row_blocks[c], slice(hd * dk, (hd + 1) * dk), slice(hd * dv, (hd + 1) * dv)

        def mix_unit(c, g):
            rows = row_blocks[c]
            gl = slice(g * gd, (g + 1) * gd)
            mix_s[rows, gl] = _dot(wsc_ref[g], vn_s[rows, gl]) + bs_ref[g]

        units = [(c, hd) for c in range(n_chunks) for hd in range(RET_HEADS)]


        for hf in halves:
            x1_s[:, hf] = xl_ref[0, rows_j, hf] + _wdot(mg_s[j], wo_ref[:, hf])
        h_s[...] = _rms(x_ref[0, rows_j, :], gm_ref[...]).astype(_BF16)
        q_raw = proj(_QK0, 0)
        hp_s[...] = _rms(x1_s[...], gp_ref[...]).astype(_BF16)
        k_raw = proj(_QK0, 1)
        pg = [_wdot(hp_s[...], wpg_ref[:, hf]) for hf in halves]
        rotary(q_raw, cq_ref[rows_j, :], sq_ref[rows_j, :], xi_ref, q_s, qx_s)
        sv0 = proj(_SV0, 0)
        rotary(k_raw, ck_ref[rows_j, :], sk_ref[rows_j, :], zeta_ref, k_s, kz_s)
        v_s[:, halves[0]] = proj(_V0, 0).astype(_BF16)
        g_s[:, halves[0]] = _gelu(sv0)
        sv1 = proj(_SV0, 1)
        v_s[:, halves[1]] = proj(_V0, 1).astype(_BF16)
        pp = _wdot(pl_ref[0, rows_j, :].astype(_BF16), wpp_ref[...])
        x2 = x1_s[...] + jnp.concatenate([jax.nn.sigmoid(g) for g in pg], axis=1) * pp
        o_ref[0, rows_j, :] = _rms(x2, gf_ref[...])
        rg0 = proj(_RG0, 0)

        scores, kv = {}, {}
        for c, hd in units:
            rows, lanes, vl = head_slices(c, hd)
            scores[c, hd] = lax.dot_general(q_s[rows, lanes], k_s[rows, lanes],
                                            (((1,), (1,)), ((), ())),
                                            preferred_element_type=_F32)
        g_s[:, halves[1]] = _gelu(sv1)
        for c, hd in units:
            rows, lanes, vl = head_slices(c, hd)
            kv[c, hd] = lax.dot_general(kz_s[rows, lanes], v_s[rows, vl],
                                        (((0,), (0,)), ((), ())),
                                        preferred_element_type=_F32)
        rg1 = proj(_RG0, 1)
        gate_s[:, halves[0]] = _silu(rg0)
        prev = {}
        for hd in range(RET_HEADS):
            st = jnp.where(tile_in_seq == 0, 0.0, state_ref[hd])
            for c in range(n_chunks):
                prev[c, hd] = st
                st = kv[c, hd] + cdec_ref[hd] * st
            state_ref[hd] = st
        r = {}
        for c, hd in units:
            rows, lanes, vl = head_slices(c, hd)
            lhs = jnp.concatenate([(scores[c, hd] * decay_ref[hd]).astype(_BF16),
                                   qx_s[rows, lanes]], axis=1)
            rhs = jnp.concatenate([v_s[rows, vl], prev[c, hd].astype(_BF16)], axis=0)
            r[c, hd] = _dot(lhs, rhs)
        gate_s[:, halves[1]] = _silu(rg1)

        su0 = proj(_SU0, 0)
        vn_s[...] = _unit(g_s[...], GN_EPS).astype(_BF16)
        su1 = proj(_SU0, 1)
        for c, hd in units:
            rows, lanes, vl = head_slices(c, hd)
            ret_s[rows, vl] = (_unit(r[c, hd], GN_EPS) * gate_s[rows, vl]).astype(_BF16)
        sg0 = proj(_SG0, 0)
        for c in range(n_chunks):
            for g in range(SGU_GROUPS):
                mix_unit(c, g)
        gu_s[:, halves[0]] = _gelu(su0)
        sg1 = proj(_SG0, 1)
        ro0 = _wdot(ret_s[...], wro_ref[:, halves[0]])
        gu_s[:, halves[1]] = _gelu(su1)
        ro1 = _wdot(ret_s[...], wro_ref[:, halves[1]])
        sgu_s[:, halves[0]] = (gu_s[:, halves[0]] * mix_s[:, halves[0]] * _silu(sg0)).astype(_BF16)
        mr0 = proj(_MR0, 0)
        sgu_s[:, halves[1]] = (gu_s[:, halves[1]] * mix_s[:, halves[1]] * _silu(sg1)).astype(_BF16)
        mr1 = proj(_MR0, 1)
        ms0 = proj(_MS0, 0)
        t_s[:, halves[0]] = jax.nn.sigmoid(mr0) * ro0
        ms1 = proj(_MS0, 1)
        t_s[:, halves[1]] = jax.nn.sigmoid(mr1) * ro1
        so0 = _wdot(sgu_s[...], wso_ref[:, halves[0]])
        sm0 = jax.nn.sigmoid(ms0)
        so1 = _wdot(sgu_s[...], wso_ref[:, halves[1]])
        sm1 = jax.nn.sigmoid(ms1)
        mg_s[j, :, halves[0]] = (t_s[:, halves[0]] + sm0 * so0).astype(_BF16)
        mg_s[j, :, halves[1]] = (t_s[:, halves[1]] + sm1 * so1).astype(_BF16)
        return carry

    @pl.when(block < n_blocks)
    def _():
        lax.fori_loop(0, tiles_per_block, tile_body, 0, unroll=True)

    @pl.when(block == n_blocks)
    def _():
        for j in range(tiles_per_block):
            out_only(j)


def _position_tables(seq, ts):
    dk = RET_QK_DIM
    half = dk // 2
    inv = ROPE_BASE ** (-jnp.arange(half, dtype=_F32) / half)
    ang = jnp.arange(seq, dtype=_F32)[:, None] * inv[None, :]
    cos, sin = jnp.cos(ang), jnp.sin(ang)
    cos_full = jnp.concatenate([cos, cos], axis=-1)
    sin_signed = jnp.concatenate([-sin, sin], axis=-1)
    scale = dk ** -0.5

    log_g = jnp.log(1.0 - 2.0 ** (-5.0 - jnp.arange(RET_HEADS, dtype=_F32)))
    idx = jnp.arange(CHUNK, dtype=_F32)
    diff = idx[:, None] - idx[None, :]
    decay = jnp.where(diff[None] >= 0,
                      jnp.exp(jnp.maximum(diff, 0.0)[None] * log_g[:, None, None]), 0.0)
    zeta = jnp.exp((CHUNK - 1.0 - idx)[None, :] * log_g[:, None])
    xi = jnp.exp((idx + 1.0)[None, :] * log_g[:, None])
    cdec = jnp.exp(CHUNK * log_g)
    reps = ts // CHUNK
    tile = lambda t: jnp.broadcast_to(jnp.tile(t, (1, reps))[:, :, None],
                                      (RET_HEADS, ts, dk))
    return (cos_full * scale, sin_signed * scale, cos_full, sin_signed,
            tile(xi), tile(zeta), decay,
            jnp.broadcast_to(cdec[:, None, None], (RET_HEADS, 1, RET_V_DIM)))


def _const_spec(shape):
    zeros = (0,) * len(shape)
    return pl.BlockSpec(shape, lambda i: zeros, pipeline_mode=pl.Buffered(1))


@jax.jit
def kernel(x, p, w_in, w_ret_out, w_sgu_out, w_out, sgu_ws, sgu_bs, w_ple_gate,
           w_ple_proj, g_mixer, g_ple, g_final):
    batch, seq, d = x.shape
    depth = w_in.shape[0]
    ts = TOKENS_PER_TILE
    bt = ts * TILES_PER_STEP
    assert depth == 1 and d == D_MODEL and seq % bt == 0 and ts % CHUNK == 0
    n_blocks = batch * seq // bt
    cq, sq, ck, sk, xi, zeta, decay, cdec = _position_tables(seq, ts)
    tril = jnp.tril(jnp.ones((CHUNK, CHUNK), _F32))

    x_blocks = x.reshape(n_blocks, bt, d)
    p_blocks = p[0].reshape(n_blocks, bt, PLE_DIM)
    cur_spec = lambda width: pl.BlockSpec(
        (1, bt, width), lambda b: (jnp.minimum(b, n_blocks - 1), 0, 0))
    lag_spec = lambda width: pl.BlockSpec(
        (1, bt, width), lambda b: (jnp.maximum(b - 1, 0), 0, 0))

    ws_causal = (sgu_ws[0] * tril[None]).astype(_BF16)
    bs_rows = jnp.broadcast_to(sgu_bs[0][:, :, None],
                               (SGU_GROUPS, CHUNK, SGU_GROUP_DIM)).astype(_F32)
    (win_p,) = _pack_rows(w_in[0])
    wro_p, wso_p, wo_p, wpg_p, wpp_p = _pack_rows(
        w_ret_out[0], w_sgu_out[0], w_out[0], w_ple_gate[0], w_ple_proj[0])
    blocks_per_seq = seq // bt
    pos_spec = pl.BlockSpec(
        (bt, RET_QK_DIM), lambda b: (lax.rem(jnp.minimum(b, n_blocks - 1), blocks_per_seq), 0))
    consts = (xi, zeta, decay, cdec,
              win_p, wro_p, wso_p, wo_p, ws_causal, bs_rows, wpg_p, wpp_p,
              g_mixer[0][None, :], g_ple[0][None, :], g_final[None, :])
    act = lambda width, dt: pltpu.VMEM((ts, width), dt)
    out = pl.pallas_call(
        functools.partial(_block_kernel, seq // ts, n_blocks),
        grid=(n_blocks + 1,),
        in_specs=[cur_spec(D_MODEL), lag_spec(D_MODEL), lag_spec(PLE_DIM)]
                 + [pos_spec] * 4 + [_const_spec(c.shape) for c in consts],
        out_specs=lag_spec(D_MODEL),
        out_shape=jax.ShapeDtypeStruct(x_blocks.shape, x.dtype),
        scratch_shapes=[
            pltpu.VMEM((RET_HEADS, RET_QK_DIM, RET_V_DIM), _F32),
            act(D_MODEL, _BF16),
            act(RET_HEADS * RET_QK_DIM, _BF16),
            act(RET_HEADS * RET_QK_DIM, _BF16),
            act(RET_HEADS * RET_QK_DIM, _BF16),
            act(RET_HEADS * RET_QK_DIM, _BF16),
            act(D_MODEL, _BF16),
            act(D_MODEL, _F32),
            act(D_MODEL, _BF16),
            act(D_MODEL, _F32),
            act(D_MODEL, _BF16),
            act(D_MODEL, _F32),
            act(D_MODEL, _F32),
            act(D_MODEL, _BF16),
            act(D_MODEL, _F32),
            pltpu.VMEM((TILES_PER_STEP, ts, D_MODEL), _BF16),
            act(D_MODEL, _F32),
            act(D_MODEL, _BF16),
        ],
        compiler_params=pltpu.CompilerParams(
            dimension_semantics=("arbitrary",),
            vmem_limit_bytes=V7X_VMEM_LIMIT_BYTES),
        name="hybrid_block",
    )(x_blocks, x_blocks, p_blocks, cq, sq, ck, sk, *consts)
    return out.reshape(x.shape)
```

```python
import functools

import jax
import jax.numpy as jnp
import numpy as np
from jax import lax
from jax.experimental import pallas as pl
from jax.experimental.pallas import tpu as pltpu

D_MODEL = 1024
PLE_DIM = 256
CHUNK = 128
RET_HEADS = 4
RET_V_DIM = D_MODEL // RET_HEADS
RET_QK_DIM = RET_V_DIM // 2
SGU_GROUPS = 4
SGU_GROUP_DIM = D_MODEL // SGU_GROUPS
ROPE_BASE = 10000.0
NORM_EPS = 1e-6
GN_EPS = 1e-5

_QK0 = 0
_V0 = 2 * RET_HEADS * RET_QK_DIM
_RG0 = _V0 + D_MODEL
_SU0 = _RG0 + D_MODEL
_SV0 = _SU0 + D_MODEL
_SG0 = _SV0 + D_MODEL
_MR0 = _SG0 + D_MODEL
_MS0 = _MR0 + D_MODEL
IN_WIDTH = _MS0 + D_MODEL

TOKENS_PER_TILE = 256
TILES_PER_STEP = 2
V7X_VMEM_LIMIT_BYTES = 60 * 1024 * 1024
PACK_BLOCK_ROWS = 512
PACK_BLOCK_COLS = 2048
PACKED_ROW_ALIGN = 16

_F32 = jnp.float32
_BF16 = jnp.bfloat16


def _dot(a, b):
    return jnp.dot(a, b, preferred_element_type=_F32)


def _pack_kernel(*refs):
    n = len(refs) // 2
    for w_ref, o_ref in zip(refs[:n], refs[n:]):
        o_ref[...] = pltpu.bitcast(w_ref[...].astype(_BF16), jnp.uint32)


def _pack_rows(*ws):
    n = ws[0].shape[1]
    nb = min(n, PACK_BLOCK_COLS)
    slabs = max(w.shape[0] for w in ws) // PACK_BLOCK_ROWS
    assert n % nb == 0
    assert all(w.shape[1] == n and w.shape[0] % (PACKED_ROW_ALIGN * slabs) == 0 for w in ws)
    kbs = [w.shape[0] // slabs for w in ws]
    return pl.pallas_call(
        _pack_kernel,
        grid=(slabs, n // nb),
        in_specs=[pl.BlockSpec((kb, nb), lambda i, j: (i, j)) for kb in kbs],
        out_specs=[pl.BlockSpec((kb // 2, nb), lambda i, j: (i, j)) for kb in kbs],
        out_shape=[jax.ShapeDtypeStruct((w.shape[0] // 2, n), jnp.uint32) for w in ws],
        name="pack_weight_rows",
    )(*ws)


def _wdot(a, w_packed):
    return _dot(a, pltpu.bitcast(w_packed, _BF16))


def _rms(x, g):
    ms = jnp.mean(x * x, axis=-1, keepdims=True)
    return x * lax.rsqrt(ms + NORM_EPS) * g


def _unit(x, eps):
    mu = jnp.mean(x, axis=-1, keepdims=True)
    d = x - mu
    var = jnp.mean(d * d, axis=-1, keepdims=True)
    return d * lax.rsqrt(var + eps)


def _silu(x):
    return x * jax.nn.sigmoid(x)


def _gelu(x):
    return 0.5 * x * (1.0 + lax.erf(x * (0.5 ** 0.5)))


def _block_kernel(tiles_per_seq, n_blocks,
                  x_ref, xl_ref, pl_ref, cq_ref, sq_ref, ck_ref, sk_ref, xi_ref, zeta_ref,
                  decay_ref, cdec_ref, win_ref, wro_ref, wso_ref, wo_ref, wsc_ref,
                  bs_ref, wpg_ref, wpp_ref, gm_ref, gp_ref, gf_ref,
                  o_ref,
                  state_ref, h_s, q_s, qx_s, k_s, kz_s, v_s, gate_s, ret_s, g_s, vn_s,
                  gu_s, mix_s, sgu_s, t_s, mg_s, x1_s, hp_s):
    ts = TOKENS_PER_TILE
    tiles_per_block = x_ref.shape[1] // ts
    n_chunks = ts // CHUNK
    dk, dv, gd = RET_QK_DIM, RET_V_DIM, SGU_GROUP_DIM
    half = D_MODEL // 2
    halves = (slice(0, half), slice(half, D_MODEL))
    row_blocks = [slice(c * CHUNK, (c + 1) * CHUNK) for c in range(n_chunks)]
    block = pl.program_id(0)

    @pl.when(block == 0)
    def _():
        mg_s[...] = jnp.zeros_like(mg_s)

    def out_only(j):
        rows_j = pl.ds(j * ts, ts)
        for hf in halves:
            x1_s[:, hf] = xl_ref[0, rows_j, hf] + _wdot(mg_s[j], wo_ref[:, hf])
        hp = _rms(x1_s[...], gp_ref[...]).astype(_BF16)
        pg = [_wdot(hp, wpg_ref[:, hf]) for hf in halves]
        pp = _wdot(pl_ref[0, rows_j, :].astype(_BF16), wpp_ref[...])
        x2 = x1_s[...] + jnp.concatenate([jax.nn.sigmoid(g) for g in pg], axis=1) * pp
        o_ref[0, rows_j, :] = _rms(x2, gf_ref[...])

    def tile_body(j, carry):
        tile = block * tiles_per_block + j
        tile_in_seq = lax.rem(tile, tiles_per_seq)
        rows_j = pl.ds(pl.multiple_of(j * ts, ts), ts)

        def proj(col0, hf):
            c0 = col0 + hf * half
            return _wdot(h_s[...], win_ref[:, c0:c0 + half])

        def rotary(val, cos, sin, row_factor_ref, plain_s, scaled_s):
            for hd in range(RET_HEADS):
                lanes = slice(hd * dk, (hd + 1) * dk)
                xh = val[:, lanes]
                xr = xh * cos + pltpu.roll(xh, dk // 2, 1) * sin
                plain_s[:, lanes] = xr.astype(_BF16)
                scaled_s[:, lanes] = (xr * row_factor_ref[hd]).astype(_BF16)

        def head_slices(c, hd):
            return row_blocks[c], slice(hd * dk, (hd + 1) * dk), slice(hd * dv, (hd + 1) * dv)

        def mix_unit(c, g):
            rows = row_blocks[c]
            gl = slice(g * gd, (g + 1) * gd)
            mix_s[rows, gl] = _dot(wsc_ref[g], vn_s[rows, gl]) + bs_ref[g]

        units = [(c, hd) for c in range(n_chunks) for hd in range(RET_HEADS)]


        for hf in halves:
            x1_s[:, hf] = xl_ref[0, rows_j, hf] + _wdot(mg_s[j], wo_ref[:, hf])
        h_s[...] = _rms(x_ref[0, rows_j, :], gm_ref[...]).astype(_BF16)
        q_raw = proj(_QK0, 0)
        hp_s[...] = _rms(x1_s[...], gp_ref[...]).astype(_BF16)
        k_raw = proj(_QK0, 1)
        pg = [_wdot(hp_s[...], wpg_ref[:, hf]) for hf in halves]
        rotary(q_raw, cq_ref[rows_j, :], sq_ref[rows_j, :], xi_ref, q_s, qx_s)
        sv0 = proj(_SV0, 0)
        rotary(k_raw, ck_ref[rows_j, :], sk_ref[rows_j, :], zeta_ref, k_s, kz_s)
        v_s[:, halves[0]] = proj(_V0, 0).astype(_BF16)
        g_s[:, halves[0]] = _gelu(sv0)
        sv1 = proj(_SV0, 1)
        v_s[:, halves[1]] = proj(_V0, 1).astype(_BF16)
        pp = _wdot(pl_ref[0, rows_j, :].astype(_BF16), wpp_ref[...])
        x2 = x1_s[...] + jnp.concatenate([jax.nn.sigmoid(g) for g in pg], axis=1) * pp
        o_ref[0, rows_j, :] = _rms(x2, gf_ref[...])
        rg0 = proj(_RG0, 0)

        scores, kv = {}, {}
        for c, hd in units:
            rows, lanes, vl = head_slices(c, hd)
            scores[c, hd] = lax.dot_general(q_s[rows, lanes], k_s[rows, lanes],
                                            (((1,), (1,)), ((), ())),
                                            preferred_element_type=_F32)
        g_s[:, halves[1]] = _gelu(sv1)
        for c, hd in units:
            rows, lanes, vl = head_slices(c, hd)
            kv[c, hd] = lax.dot_general(kz_s[rows, lanes], v_s[rows, vl],
                                        (((0,), (0,)), ((), ())),
                                        preferred_element_type=_F32)
        rg1 = proj(_RG0, 1)
        gate_s[:, halves[0]] = _silu(rg0)
        prev = {}
        for hd in range(RET_HEADS):
            st = jnp.where(tile_in_seq == 0, 0.0, state_ref[hd])
            for c in range(n_chunks):
                prev[c, hd] = st
                st = kv[c, hd] + cdec_ref[hd] * st
            state_ref[hd] = st
        r = {}
        for c, hd in units:
            rows, lanes, vl = head_slices(c, hd)
            lhs = jnp.concatenate([(scores[c, hd] * decay_ref[hd]).astype(_BF16),
                                   qx_s[rows, lanes]], axis=1)
            rhs = jnp.concatenate([v_s[rows, vl], prev[c, hd].astype(_BF16)], axis=0)
            r[c, hd] = _dot(lhs, rhs)
        gate_s[:, halves[1]] = _silu(rg1)

        su0 = proj(_SU0, 0)
        vn_s[...] = _unit(g_s[...], GN_EPS).astype(_BF16)
        su1 = proj(_SU0, 1)
        for c, hd in units:
            rows, lanes, vl = head_slices(c, hd)
            ret_s[rows, vl] = (_unit(r[c, hd], GN_EPS) * gate_s[rows, vl]).astype(_BF16)
        sg0 = proj(_SG0, 0)
        for c in range(n_chunks):
            for g in range(SGU_GROUPS):
                mix_unit(c, g)
        gu_s[:, halves[0]] = _gelu(su0)
        sg1 = proj(_SG0, 1)
        ro0 = _wdot(ret_s[...], wro_ref[:, halves[0]])
        gu_s[:, halves[1]] = _gelu(su1)
        ro1 = _wdot(ret_s[...], wro_ref[:, halves[1]])
        sgu_s[:, halves[0]] = (gu_s[:, halves[0]] * mix_s[:, halves[0]] * _silu(sg0)).astype(_BF16)
        mr0 = proj(_MR0, 0)
        sgu_s[:, halves[1]] = (gu_s[:, halves[1]] * mix_s[:, halves[1]] * _silu(sg1)).astype(_BF16)
        mr1 = proj(_MR0, 1)
        ms0 = proj(_MS0, 0)
        t_s[:, halves[0]] = jax.nn.sigmoid(mr0) * ro0
        ms1 = proj(_MS0, 1)
        t_s[:, halves[1]] = jax.nn.sigmoid(mr1) * ro1
        so0 = _wdot(sgu_s[...], wso_ref[:, halves[0]])
        sm0 = jax.nn.sigmoid(ms0)
        so1 = _wdot(sgu_s[...], wso_ref[:, halves[1]])
        sm1 = jax.nn.sigmoid(ms1)
        mg_s[j, :, halves[0]] = (t_s[:, halves[0]] + sm0 * so0).astype(_BF16)
        mg_s[j, :, halves[1]] = (t_s[:, halves[1]] + sm1 * so1).astype(_BF16)
        return carry

    @pl.when(block < n_blocks)
    def _():
        lax.fori_loop(0, tiles_per_block, tile_body, 0, unroll=True)

    @pl.when(block == n_blocks)
    def _():
        for j in range(tiles_per_block):
            out_only(j)


def _position_tables(seq, ts):
    dk = RET_QK_DIM
    half = dk // 2
    inv = ROPE_BASE ** (-np.arange(half, dtype=np.float64) / half)
    ang = np.arange(seq, dtype=np.float64)[:, None] * inv[None, :]
    cos, sin = np.cos(ang), np.sin(ang)
    cos_full = np.concatenate([cos, cos], axis=-1)
    sin_signed = np.concatenate([-sin, sin], axis=-1)
    scale = dk ** -0.5

    log_g = np.log(1.0 - 2.0 ** (-5.0 - np.arange(RET_HEADS, dtype=np.float64)))
    idx = np.arange(CHUNK, dtype=np.float64)
    diff = idx[:, None] - idx[None, :]
    decay = np.where(diff[None] >= 0,
                     np.exp(np.maximum(diff, 0.0)[None] * log_g[:, None, None]), 0.0)
    zeta = np.exp((CHUNK - 1.0 - idx)[None, :] * log_g[:, None])
    xi = np.exp((idx + 1.0)[None, :] * log_g[:, None])
    cdec = np.exp(CHUNK * log_g)
    reps = ts // CHUNK
    tile = lambda t: np.broadcast_to(np.tile(t, (1, reps))[:, :, None],
                                     (RET_HEADS, ts, dk))
    tables = (cos_full * scale, sin_signed * scale, cos_full, sin_signed,
              tile(xi), tile(zeta), decay,
              np.broadcast_to(cdec[:, None, None], (RET_HEADS, 1, RET_V_DIM)))
    return tuple(jnp.asarray(np.ascontiguousarray(t, dtype=np.float32)) for t in tables)


def _const_spec(shape):
    zeros = (0,) * len(shape)
    return pl.BlockSpec(shape, lambda i: zeros, pipeline_mode=pl.Buffered(1))


@jax.jit
def kernel(x, p, w_in, w_ret_out, w_sgu_out, w_out, sgu_ws, sgu_bs, w_ple_gate,
           w_ple_proj, g_mixer, g_ple, g_final):
    batch, seq, d = x.shape
    depth = w_in.shape[0]
    ts = TOKENS_PER_TILE
    bt = ts * TILES_PER_STEP
    assert depth == 1 and d == D_MODEL and seq % bt == 0 and ts % CHUNK == 0
    n_blocks = batch * seq // bt
    cq, sq, ck, sk, xi, zeta, decay, cdec = _position_tables(seq, ts)
    tril = jnp.tril(jnp.ones((CHUNK, CHUNK), _F32))

    x_blocks = x.reshape(n_blocks, bt, d)
    p_blocks = p[0].reshape(n_blocks, bt, PLE_DIM)
    cur_spec = lambda width: pl.BlockSpec(
        (1, bt, width), lambda b: (jnp.minimum(b, n_blocks - 1), 0, 0))
    lag_spec = lambda width: pl.BlockSpec(
        (1, bt, width), lambda b: (jnp.maximum(b - 1, 0), 0, 0))

    ws_causal = (sgu_ws[0] * tril[None]).astype(_BF16)
    bs_rows = jnp.broadcast_to(sgu_bs[0][:, :, None],
                               (SGU_GROUPS, CHUNK, SGU_GROUP_DIM)).astype(_F32)
    (win_p,) = _pack_rows(w_in[0])
    wro_p, wso_p, wo_p, wpg_p, wpp_p = _pack_rows(
        w_ret_out[0], w_sgu_out[0], w_out[0], w_ple_gate[0], w_ple_proj[0])
    blocks_per_seq = seq // bt
    pos_spec = pl.BlockSpec(
        (bt, RET_QK_DIM), lambda b: (lax.rem(jnp.minimum(b, n_blocks - 1), blocks_per_seq), 0))
    consts = (xi, zeta, decay, cdec,
              win_p, wro_p, wso_p, wo_p, ws_causal, bs_rows, wpg_p, wpp_p,
              g_mixer[0][None, :], g_ple[0][None, :], g_final[None, :])
    act = lambda width, dt: pltpu.VMEM((ts, width), dt)
    out = pl.pallas_call(
        functools.partial(_block_kernel, seq // ts, n_blocks),
        grid=(n_blocks + 1,),
        in_specs=[cur_spec(D_MODEL), lag_spec(D_MODEL), lag_spec(PLE_DIM)]
                 + [pos_spec] * 4 + [_const_spec(c.shape) for c in consts],
        out_specs=lag_spec(D_MODEL),
        out_shape=jax.ShapeDtypeStruct(x_blocks.shape, x.dtype),
        scratch_shapes=[
            pltpu.VMEM((RET_HEADS, RET_QK_DIM, RET_V_DIM), _F32),
            act(D_MODEL, _BF16),
            act(RET_HEADS * RET_QK_DIM, _BF16),
            act(RET_HEADS * RET_QK_DIM, _BF16),
            act(RET_HEADS * RET_QK_DIM, _BF16),
            act(RET_HEADS * RET_QK_DIM, _BF16),
            act(D_MODEL, _BF16),
            act(D_MODEL, _F32),
            act(D_MODEL, _BF16),
            act(D_MODEL, _F32),
            act(D_MODEL, _BF16),
            act(D_MODEL, _F32),
            act(D_MODEL, _F32),
            act(D_MODEL, _BF16),
            act(D_MODEL, _F32),
            pltpu.VMEM((TILES_PER_STEP, ts, D_MODEL), _BF16),
            act(D_MODEL, _F32),
            act(D_MODEL, _BF16),
        ],
        compiler_params=pltpu.CompilerParams(
            dimension_semantics=("arbitrary",),
            vmem_limit_bytes=V7X_VMEM_LIMIT_BYTES),
        name="hybrid_block",
    )(x_blocks, x_blocks, p_blocks, cq, sq, ck, sk, *consts)
    return out.reshape(x.shape)
```

```python
import functools

import jax
import jax.numpy as jnp
import numpy as np
from jax import lax
from jax.experimental import pallas as pl
from jax.experimental.pallas import tpu as pltpu

D_MODEL = 1024
PLE_DIM = 256
CHUNK = 128
RET_HEADS = 4
RET_V_DIM = D_MODEL // RET_HEADS
RET_QK_DIM = RET_V_DIM // 2
SGU_GROUPS = 4
SGU_GROUP_DIM = D_MODEL // SGU_GROUPS
ROPE_BASE = 10000.0
NORM_EPS = 1e-6
GN_EPS = 1e-5

_QK0 = 0
_V0 = 2 * RET_HEADS * RET_QK_DIM
_RG0 = _V0 + D_MODEL
_SU0 = _RG0 + D_MODEL
_SV0 = _SU0 + D_MODEL
_SG0 = _SV0 + D_MODEL
_MR0 = _SG0 + D_MODEL
_MS0 = _MR0 + D_MODEL
IN_WIDTH = _MS0 + D_MODEL

TOKENS_PER_TILE = 256
TILES_PER_STEP = 2
V7X_VMEM_LIMIT_BYTES = 60 * 1024 * 1024
WEIGHT_SLAB_ROWS = 256
WEIGHT_SLAB_COLS = 2048

_F32 = jnp.float32
_BF16 = jnp.bfloat16


def _dot(a, b):
    return jnp.dot(a, b, preferred_element_type=_F32)


def _wdot(a, w_packed):
    return _dot(a, pltpu.bitcast(w_packed, _BF16))


def _rms(x, g):
    ms = jnp.mean(x * x, axis=-1, keepdims=True)
    return x * lax.rsqrt(ms + NORM_EPS) * g


def _unit(x, eps):
    mu = jnp.mean(x, axis=-1, keepdims=True)
    d = x - mu
    var = jnp.mean(d * d, axis=-1, keepdims=True)
    return d * lax.rsqrt(var + eps)


def _silu(x):
    return x * jax.nn.sigmoid(x)


def _gelu(x):
    return 0.5 * x * (1.0 + lax.erf(x * (0.5 ** 0.5)))


def _block_kernel(tiles_per_seq, n_blocks,
                  x_ref, xl_ref, pl_ref, cq_ref, sq_ref, ck_ref, sk_ref, xi_ref, zeta_ref,
                  decay_ref, cdec_ref, wsc_ref, bs_ref, gm_ref, gp_ref, gf_ref,
                  win_hbm, wro_hbm, wso_hbm, wo_hbm, wpg_hbm, wpp_hbm,
                  o_ref,
                  win_ref, wro_ref, wso_ref, wo_ref, wpg_ref, wpp_ref, stage_ref, stage_sem,
                  state_ref, h_s, q_s, qx_s, k_s, kz_s, v_s, gate_s, ret_s, g_s, vn_s,
                  gu_s, mix_s, sgu_s, t_s, mg_s, x1_s, hp_s):
    ts = TOKENS_PER_TILE
    tiles_per_block = x_ref.shape[1] // ts
    n_chunks = ts // CHUNK
    dk, dv, gd = RET_QK_DIM, RET_V_DIM, SGU_GROUP_DIM
    half = D_MODEL // 2
    halves = (slice(0, half), slice(half, D_MODEL))
    row_blocks = [slice(c * CHUNK, (c + 1) * CHUNK) for c in range(n_chunks)]
    block = pl.program_id(0)

    @pl.when(block == 0)
    def _():
        mg_s[...] = jnp.zeros_like(mg_s)
        slabs = []
        for src_hbm, dst_ref in ((win_hbm, win_ref), (wro_hbm, wro_ref), (wso_hbm, wso_ref),
                                 (wo_hbm, wo_ref), (wpg_hbm, wpg_ref), (wpp_hbm, wpp_ref)):
            k, n = src_hbm.shape
            rows, cols = min(k, WEIGHT_SLAB_ROWS), min(n, WEIGHT_SLAB_COLS)
            for r0 in range(0, k, rows):
                for c0 in range(0, n, cols):
                    slabs.append((src_hbm, dst_ref, r0, c0, rows, cols))

        def slab_copy(i):
            src_hbm, _, r0, c0, rows, cols = slabs[i]
            return pltpu.make_async_copy(
                src_hbm.at[pl.ds(r0, rows), pl.ds(c0, cols)],
                stage_ref.at[i % 2, pl.ds(0, rows), pl.ds(0, cols)],
                stage_sem.at[i % 2])

        slab_copy(0).start()
        for i, (_, dst_ref, r0, c0, rows, cols) in enumerate(slabs):
            if i + 1 < len(slabs):
                slab_copy(i + 1).start()
            slab_copy(i).wait()
            slab = stage_ref[i % 2, 0:rows, 0:cols].astype(_BF16)
            dst_ref[r0 // 2:(r0 + rows) // 2, c0:c0 + cols] = pltpu.bitcast(slab, jnp.uint32)

    def out_only(j):
        rows_j = pl.ds(j * ts, ts)
        for hf in halves:
            x1_s[:, hf] = xl_ref[0, rows_j, hf] + _wdot(mg_s[j], wo_ref[:, hf])
        hp = _rms(x1_s[...], gp_ref[...]).astype(_BF16)
        pg = [_wdot(hp, wpg_ref[:, hf]) for hf in halves]
        pp = _wdot(pl_ref[0, rows_j, :].astype(_BF16), wpp_ref[...])
        x2 = x1_s[...] + jnp.concatenate([jax.nn.sigmoid(g) for g in pg], axis=1) * pp
        o_ref[0, rows_j, :] = _rms(x2, gf_ref[...])

    def tile_body(j, carry):
        tile = block * tiles_per_block + j
        tile_in_seq = lax.rem(tile, tiles_per_seq)
        rows_j = pl.ds(pl.multiple_of(j * ts, ts), ts)

        def proj(col0, hf):
            c0 = col0 + hf * half
            return _wdot(h_s[...], win_ref[:, c0:c0 + half])

        def rotary(val, cos, sin, row_factor_ref, plain_s, scaled_s):
            for hd in range(RET_HEADS):
                lanes = slice(hd * dk, (hd + 1) * dk)
                xh = val[:, lanes]
                xr = xh * cos + pltpu.roll(xh, dk // 2, 1) * sin
                plain_s[:, lanes] = xr.astype(_BF16)
                scaled_s[:, lanes] = (xr * row_factor_ref[hd]).astype(_BF16)

        def head_slices(c, hd):
            return row_blocks[c], slice(hd * dk, (hd + 1) * dk), slice(hd * dv, (hd + 1) * dv)

        def mix_unit(c, g):
            rows = row_blocks[c]
            gl = slice(g * gd, (g + 1) * gd)
            mix_s[rows, gl] = _dot(wsc_ref[g], vn_s[rows, gl]) + bs_ref[g]

        units = [(c, hd) for c in range(n_chunks) for hd in range(RET_HEADS)]


        for hf in halves:
            x1_s[:, hf] = xl_ref[0, rows_j, hf] + _wdot(mg_s[j], wo_ref[:, hf])
        h_s[...] = _rms(x_ref[0, rows_j, :], gm_ref[...]).astype(_BF16)
        q_raw = proj(_QK0, 0)
        hp_s[...] = _rms(x1_s[...], gp_ref[...]).astype(_BF16)
        k_raw = proj(_QK0, 1)
        pg = [_wdot(hp_s[...], wpg_ref[:, hf]) for hf in halves]
        rotary(q_raw, cq_ref[rows_j, :], sq_ref[rows_j, :], xi_ref, q_s, qx_s)
        sv0 = proj(_SV0, 0)
        rotary(k_raw, ck_ref[rows_j, :], sk_ref[rows_j, :], zeta_ref, k_s, kz_s)
        v_s[:, halves[0]] = proj(_V0, 0).astype(_BF16)
        g_s[:, halves[0]] = _gelu(sv0)
        sv1 = proj(_SV0, 1)
        v_s[:, halves[1]] = proj(_V0, 1).astype(_BF16)
        pp = _wdot(pl_ref[0, rows_j, :].astype(_BF16), wpp_ref[...])
        x2 = x1_s[...] + jnp.concatenate([jax.nn.sigmoid(g) for g in pg], axis=1) * pp
        o_ref[0, rows_j, :] = _rms(x2, gf_ref[...])
        rg0 = proj(_RG0, 0)

        scores, kv = {}, {}
        for c, hd in units:
            rows, lanes, vl = head_slices(c, hd)
            scores[c, hd] = lax.dot_general(q_s[rows, lanes], k_s[rows, lanes],
                                            (((1,), (1,)), ((), ())),
                                            preferred_element_type=_F32)
        g_s[:, halves[1]] = _gelu(sv1)
        for c, hd in units:
            rows, lanes, vl = head_slices(c, hd)
            kv[c, hd] = lax.dot_general(kz_s[rows, lanes], v_s[rows, vl],
                                        (((0,), (0,)), ((), ())),
                                        preferred_element_type=_F32)
        rg1 = proj(_RG0, 1)
        gate_s[:, halves[0]] = _silu(rg0)
        prev = {}
        for hd in range(RET_HEADS):
            st = jnp.where(tile_in_seq == 0, 0.0, state_ref[hd])
            for c in range(n_chunks):
                prev[c, hd] = st
                st = kv[c, hd] + cdec_ref[hd] * st
            state_ref[hd] = st
        r = {}
        for c, hd in units:
            rows, lanes, vl = head_slices(c, hd)
            lhs = jnp.concatenate([(scores[c, hd] * decay_ref[hd]).astype(_BF16),
                                   qx_s[rows, lanes]], axis=1)
            rhs = jnp.concatenate([v_s[rows, vl], prev[c, hd].astype(_BF16)], axis=0)
            r[c, hd] = _dot(lhs, rhs)
        gate_s[:, halves[1]] = _silu(rg1)

        su0 = proj(_SU0, 0)
        vn_s[...] = _unit(g_s[...], GN_EPS).astype(_BF16)
        su1 = proj(_SU0, 1)
        for c, hd in units:
            rows, lanes, vl = head_slices(c, hd)
            ret_s[rows, vl] = (_unit(r[c, hd], GN_EPS) * gate_s[rows, vl]).astype(_BF16)
        sg0 = proj(_SG0, 0)
        for c in range(n_chunks):
            for g in range(SGU_GROUPS):
                mix_unit(c, g)
        gu_s[:, halves[0]] = _gelu(su0)
        sg1 = proj(_SG0, 1)
        ro0 = _wdot(ret_s[...], wro_ref[:, halves[0]])
        gu_s[:, halves[1]] = _gelu(su1)
        ro1 = _wdot(ret_s[...], wro_ref[:, halves[1]])
        sgu_s[:, halves[0]] = (gu_s[:, halves[0]] * mix_s[:, halves[0]] * _silu(sg0)).astype(_BF16)
        mr0 = proj(_MR0, 0)
        sgu_s[:, halves[1]] = (gu_s[:, halves[1]] * mix_s[:, halves[1]] * _silu(sg1)).astype(_BF16)
        mr1 = proj(_MR0, 1)
        ms0 = proj(_MS0, 0)
        t_s[:, halves[0]] = jax.nn.sigmoid(mr0) * ro0
        ms1 = proj(_MS0, 1)
        t_s[:, halves[1]] = jax.nn.sigmoid(mr1) * ro1
        so0 = _wdot(sgu_s[...], wso_ref[:, halves[0]])
        sm0 = jax.nn.sigmoid(ms0)
        so1 = _wdot(sgu_s[...], wso_ref[:, halves[1]])
        sm1 = jax.nn.sigmoid(ms1)
        mg_s[j, :, halves[0]] = (t_s[:, halves[0]] + sm0 * so0).astype(_BF16)
        mg_s[j, :, halves[1]] = (t_s[:, halves[1]] + sm1 * so1).astype(_BF16)
        return carry

    @pl.when(block < n_blocks)
    def _():
        lax.fori_loop(0, tiles_per_block, tile_body, 0, unroll=True)

    @pl.when(block == n_blocks)
    def _():
        for j in range(tiles_per_block):
            out_only(j)


def _position_tables(seq, ts):
    dk = RET_QK_DIM
    half = dk // 2
    inv = ROPE_BASE ** (-np.arange(half, dtype=np.float64) / half)
    ang = np.arange(seq, dtype=np.float64)[:, None] * inv[None, :]
    cos, sin = np.cos(ang), np.sin(ang)
    cos_full = np.concatenate([cos, cos], axis=-1)
    sin_signed = np.concatenate([-sin, sin], axis=-1)
    scale = dk ** -0.5

    log_g = np.log(1.0 - 2.0 ** (-5.0 - np.arange(RET_HEADS, dtype=np.float64)))
    idx = np.arange(CHUNK, dtype=np.float64)
    diff = idx[:, None] - idx[None, :]
    decay = np.where(diff[None] >= 0,
                     np.exp(np.maximum(diff, 0.0)[None] * log_g[:, None, None]), 0.0)
    zeta = np.exp((CHUNK - 1.0 - idx)[None, :] * log_g[:, None])
    xi = np.exp((idx + 1.0)[None, :] * log_g[:, None])
    cdec = np.exp(CHUNK * log_g)
    reps = ts // CHUNK
    tile = lambda t: np.broadcast_to(np.tile(t, (1, reps))[:, :, None],
                                     (RET_HEADS, ts, dk))
    tables = (cos_full * scale, sin_signed * scale, cos_full, sin_signed,
              tile(xi), tile(zeta), decay,
              np.broadcast_to(cdec[:, None, None], (RET_HEADS, 1, RET_V_DIM)))
    return tuple(jnp.asarray(np.ascontiguousarray(t, dtype=np.float32)) for t in tables)


def _const_spec(shape):
    zeros = (0,) * len(shape)
    return pl.BlockSpec(shape, lambda i: zeros, pipeline_mode=pl.Buffered(1))


@jax.jit
def kernel(x, p, w_in, w_ret_out, w_sgu_out, w_out, sgu_ws, sgu_bs, w_ple_gate,
           w_ple_proj, g_mixer, g_ple, g_final):
    batch, seq, d = x.shape
    depth = w_in.shape[0]
    ts = TOKENS_PER_TILE
    bt = ts * TILES_PER_STEP
    assert depth == 1 and d == D_MODEL and seq % bt == 0 and ts % CHUNK == 0
    n_blocks = batch * seq // bt
    cq, sq, ck, sk, xi, zeta, decay, cdec = _position_tables(seq, ts)
    tril = jnp.tril(jnp.ones((CHUNK, CHUNK), _F32))

    x_blocks = x.reshape(n_blocks, bt, d)
    p_blocks = p[0].reshape(n_blocks, bt, PLE_DIM)
    cur_spec = lambda width: pl.BlockSpec(
        (1, bt, width), lambda b: (jnp.minimum(b, n_blocks - 1), 0, 0))
    lag_spec = lambda width: pl.BlockSpec(
        (1, bt, width), lambda b: (jnp.maximum(b - 1, 0), 0, 0))

    ws_causal = (sgu_ws[0] * tril[None]).astype(_BF16)
    bs_rows = jnp.broadcast_to(sgu_bs[0][:, :, None],
                               (SGU_GROUPS, CHUNK, SGU_GROUP_DIM)).astype(_F32)
    weights = (w_in[0], w_ret_out[0], w_sgu_out[0], w_out[0], w_ple_gate[0], w_ple_proj[0])
    packed = lambda w: pltpu.VMEM((w.shape[0] // 2, w.shape[1]), jnp.uint32)
    blocks_per_seq = seq // bt
    pos_spec = pl.BlockSpec(
        (bt, RET_QK_DIM), lambda b: (lax.rem(jnp.minimum(b, n_blocks - 1), blocks_per_seq), 0))
    consts = (xi, zeta, decay, cdec, ws_causal, bs_rows,
              g_mixer[0][None, :], g_ple[0][None, :], g_final[None, :])
    act = lambda width, dt: pltpu.VMEM((ts, width), dt)
    out = pl.pallas_call(
        functools.partial(_block_kernel, seq // ts, n_blocks),
        grid=(n_blocks + 1,),
        in_specs=[cur_spec(D_MODEL), lag_spec(D_MODEL), lag_spec(PLE_DIM)]
                 + [pos_spec] * 4 + [_const_spec(c.shape) for c in consts]
                 + [pl.BlockSpec(memory_space=pl.ANY)] * len(weights),
        out_specs=lag_spec(D_MODEL),
        out_shape=jax.ShapeDtypeStruct(x_blocks.shape, x.dtype),
        scratch_shapes=[packed(w) for w in weights] + [
            pltpu.VMEM((2, WEIGHT_SLAB_ROWS, WEIGHT_SLAB_COLS), _F32),
            pltpu.SemaphoreType.DMA((2,)),
            pltpu.VMEM((RET_HEADS, RET_QK_DIM, RET_V_DIM), _F32),
            act(D_MODEL, _BF16),
            act(RET_HEADS * RET_QK_DIM, _BF16),
            act(RET_HEADS * RET_QK_DIM, _BF16),
            act(RET_HEADS * RET_QK_DIM, _BF16),
            act(RET_HEADS * RET_QK_DIM, _BF16),
            act(D_MODEL, _BF16),
            act(D_MODEL, _F32),
            act(D_MODEL, _BF16),
            act(D_MODEL, _F32),
            act(D_MODEL, _BF16),
            act(D_MODEL, _F32),
            act(D_MODEL, _F32),
            act(D_MODEL, _BF16),
            act(D_MODEL, _F32),
            pltpu.VMEM((TILES_PER_STEP, ts, D_MODEL), _BF16),
            act(D_MODEL, _F32),
            act(D_MODEL, _BF16),
        ],
        compiler_params=pltpu.CompilerParams(
            dimension_semantics=("arbitrary",),
            vmem_limit_bytes=V7X_VMEM_LIMIT_BYTES),
        name="hybrid_block",
    )(x_blocks, x_blocks, p_blocks, cq, sq, ck, sk, *consts, *weights)
    return out.reshape(x.shape)
```

```python
import functools

import jax
import jax.numpy as jnp
import numpy as np
from jax import lax
from jax.experimental import pallas as pl
from jax.experimental.pallas import tpu as pltpu

D_MODEL = 1024
PLE_DIM = 256
CHUNK = 128
RET_HEADS = 4
RET_V_DIM = D_MODEL // RET_HEADS
RET_QK_DIM = RET_V_DIM // 2
SGU_GROUPS = 4
SGU_GROUP_DIM = D_MODEL // SGU_GROUPS
ROPE_BASE = 10000.0
NORM_EPS = 1e-6
GN_EPS = 1e-5

_QK0 = 0
_V0 = 2 * RET_HEADS * RET_QK_DIM
_RG0 = _V0 + D_MODEL
_SU0 = _RG0 + D_MODEL
_SV0 = _SU0 + D_MODEL
_SG0 = _SV0 + D_MODEL
_MR0 = _SG0 + D_MODEL
_MS0 = _MR0 + D_MODEL
IN_WIDTH = _MS0 + D_MODEL

TOKENS_PER_TILE = 256
TILES_PER_STEP = 2
V7X_VMEM_LIMIT_BYTES = 60 * 1024 * 1024
WEIGHT_SLAB_ROWS = 256
WEIGHT_SLAB_COLS = 2048

_F32 = jnp.float32
_BF16 = jnp.bfloat16


def _dot(a, b):
    return jnp.dot(a, b, preferred_element_type=_F32)


def _wdot(a, w_packed):
    return _dot(a, pltpu.bitcast(w_packed, _BF16))


def _rms(x, g):
    ms = jnp.mean(x * x, axis=-1, keepdims=True)
    return x * lax.rsqrt(ms + NORM_EPS) * g


def _unit(x, eps):
    mu = jnp.mean(x, axis=-1, keepdims=True)
    d = x - mu
    var = jnp.mean(d * d, axis=-1, keepdims=True)
    return d * lax.rsqrt(var + eps)


def _silu(x):
    return x * jax.nn.sigmoid(x)


def _gelu(x):
    return 0.5 * x * (1.0 + lax.erf(x * (0.5 ** 0.5)))


def _block_kernel(tiles_per_seq, n_blocks,
                  x_ref, xl_ref, pl_ref, cq_ref, sq_ref, ck_ref, sk_ref, xi_ref, zeta_ref,
                  decay_ref, cdec_ref, wsc_ref, bs_ref, gm_ref, gp_ref, gf_ref,
                  win_hbm, wro_hbm, wso_hbm, wo_hbm, wpg_hbm, wpp_hbm,
                  o_ref,
                  win_ref, wro_ref, wso_ref, wo_ref, wpg_ref, wpp_ref,
                  state_ref, h_s, q_s, qx_s, k_s, kz_s, v_s, gate_s, ret_s, g_s, vn_s,
                  gu_s, mix_s, sgu_s, t_s, mg_s, x1_s, hp_s, stage_ref, stage_sem):
    ts = TOKENS_PER_TILE
    tiles_per_block = x_ref.shape[1] // ts
    n_chunks = ts // CHUNK
    dk, dv, gd = RET_QK_DIM, RET_V_DIM, SGU_GROUP_DIM
    half = D_MODEL // 2
    halves = (slice(0, half), slice(half, D_MODEL))
    row_blocks = [slice(c * CHUNK, (c + 1) * CHUNK) for c in range(n_chunks)]
    block = pl.program_id(0)

    @pl.when(block == 0)
    def _():
        mg_s[...] = jnp.zeros_like(mg_s)
        slabs = []
        for src_hbm, dst_ref in ((win_hbm, win_ref), (wro_hbm, wro_ref), (wso_hbm, wso_ref),
                                 (wo_hbm, wo_ref), (wpg_hbm, wpg_ref), (wpp_hbm, wpp_ref)):
            k, n = src_hbm.shape
            rows, cols = min(k, WEIGHT_SLAB_ROWS), min(n, WEIGHT_SLAB_COLS)
            for r0 in range(0, k, rows):
                for c0 in range(0, n, cols):
                    slabs.append((src_hbm, dst_ref, r0, c0, rows, cols))

        def slab_copy(i):
            src_hbm, _, r0, c0, rows, cols = slabs[i]
            return pltpu.make_async_copy(
                src_hbm.at[pl.ds(r0, rows), pl.ds(c0, cols)],
                stage_ref.at[i % 2, pl.ds(0, rows), pl.ds(0, cols)],
                stage_sem.at[i % 2])

        slab_copy(0).start()
        for i, (_, dst_ref, r0, c0, rows, cols) in enumerate(slabs):
            if i + 1 < len(slabs):
                slab_copy(i + 1).start()
            slab_copy(i).wait()
            slab = stage_ref[i % 2, 0:rows, 0:cols].astype(_BF16)
            dst_ref[r0 // 2:(r0 + rows) // 2, c0:c0 + cols] = pltpu.bitcast(slab, jnp.uint32)

    def out_only(j):
        rows_j = pl.ds(j * ts, ts)
        for hf in halves:
            x1_s[:, hf] = xl_ref[0, rows_j, hf] + _wdot(mg_s[j], wo_ref[:, hf])
        hp = _rms(x1_s[...], gp_ref[...]).astype(_BF16)
        pg = [_wdot(hp, wpg_ref[:, hf]) for hf in halves]
        pp = _wdot(pl_ref[0, rows_j, :].astype(_BF16), wpp_ref[...])
        x2 = x1_s[...] + jnp.concatenate([jax.nn.sigmoid(g) for g in pg], axis=1) * pp
        o_ref[0, rows_j, :] = _rms(x2, gf_ref[...])

    def tile_body(j, carry):
        tile = block * tiles_per_block + j
        tile_in_seq = lax.rem(tile, tiles_per_seq)
        rows_j = pl.ds(pl.multiple_of(j * ts, ts), ts)

        def proj(col0, hf):
            c0 = col0 + hf * half
            return _wdot(h_s[...], win_ref[:, c0:c0 + half])

        def rotary(val, cos, sin, row_factor_ref, plain_s, scaled_s):
            for hd in range(RET_HEADS):
                lanes = slice(hd * dk, (hd + 1) * dk)
                xh = val[:, lanes]
                xr = xh * cos + pltpu.roll(xh, dk // 2, 1) * sin
                plain_s[:, lanes] = xr.astype(_BF16)
                scaled_s[:, lanes] = (xr * row_factor_ref[hd]).astype(_BF16)

        def head_slices(c, hd):
            return row_blocks[c], slice(hd * dk, (hd + 1) * dk), slice(hd * dv, (hd + 1) * dv)

        def mix_unit(c, g):
            rows = row_blocks[c]
            gl = slice(g * gd, (g + 1) * gd)
            mix_s[rows, gl] = _dot(wsc_ref[g], vn_s[rows, gl]) + bs_ref[g]

        units = [(c, hd) for c in range(n_chunks) for hd in range(RET_HEADS)]


        for hf in halves:
            x1_s[:, hf] = xl_ref[0, rows_j, hf] + _wdot(mg_s[j], wo_ref[:, hf])
        h_s[...] = _rms(x_ref[0, rows_j, :], gm_ref[...]).astype(_BF16)
        q_raw = proj(_QK0, 0)
        hp_s[...] = _rms(x1_s[...], gp_ref[...]).astype(_BF16)
        k_raw = proj(_QK0, 1)
        pg = [_wdot(hp_s[...], wpg_ref[:, hf]) for hf in halves]
        rotary(q_raw, cq_ref[rows_j, :], sq_ref[rows_j, :], xi_ref, q_s, qx_s)
        sv0 = proj(_SV0, 0)
        rotary(k_raw, ck_ref[rows_j, :], sk_ref[rows_j, :], zeta_ref, k_s, kz_s)
        v_s[:, halves[0]] = proj(_V0, 0).astype(_BF16)
        g_s[:, halves[0]] = _gelu(sv0)
        sv1 = proj(_SV0, 1)
        v_s[:, halves[1]] = proj(_V0, 1).astype(_BF16)
        pp = _wdot(pl_ref[0, rows_j, :].astype(_BF16), wpp_ref[...])
        x2 = x1_s[...] + jnp.concatenate([jax.nn.sigmoid(g) for g in pg], axis=1) * pp
        o_ref[0, rows_j, :] = _rms(x2, gf_ref[...])
        rg0 = proj(_RG0, 0)

        scores, kv = {}, {}
        for c, hd in units:
            rows, lanes, vl = head_slices(c, hd)
            scores[c, hd] = lax.dot_general(q_s[rows, lanes], k_s[rows, lanes],
                                            (((1,), (1,)), ((), ())),
                                            preferred_element_type=_F32)
        g_s[:, halves[1]] = _gelu(sv1)
        for c, hd in units:
            rows, lanes, vl = head_slices(c, hd)
            kv[c, hd] = lax.dot_general(kz_s[rows, lanes], v_s[rows, vl],
                                        (((0,), (0,)), ((), ())),
                                        preferred_element_type=_F32)
        rg1 = proj(_RG0, 1)
        gate_s[:, halves[0]] = _silu(rg0)
        prev = {}
        for hd in range(RET_HEADS):
            st = jnp.where(tile_in_seq == 0, 0.0, state_ref[hd])
            for c in range(n_chunks):
                prev[c, hd] = st
                st = kv[c, hd] + cdec_ref[hd] * st
            state_ref[hd] = st
        r = {}
        for c, hd in units:
            rows, lanes, vl = head_slices(c, hd)
            lhs = jnp.concatenate([(scores[c, hd] * decay_ref[hd]).astype(_BF16),
                                   qx_s[rows, lanes]], axis=1)
            rhs = jnp.concatenate([v_s[rows, vl], prev[c, hd].astype(_BF16)], axis=0)
            r[c, hd] = _dot(lhs, rhs)
        gate_s[:, halves[1]] = _silu(rg1)

        su0 = proj(_SU0, 0)
        vn_s[...] = _unit(g_s[...], GN_EPS).astype(_BF16)
        su1 = proj(_SU0, 1)
        for c, hd in units:
            rows, lanes, vl = head_slices(c, hd)
            ret_s[rows, vl] = (_unit(r[c, hd], GN_EPS) * gate_s[rows, vl]).astype(_BF16)
        sg0 = proj(_SG0, 0)
        for c in range(n_chunks):
            for g in range(SGU_GROUPS):
                mix_unit(c, g)
        gu_s[:, halves[0]] = _gelu(su0)
        sg1 = proj(_SG0, 1)
        ro0 = _wdot(ret_s[...], wro_ref[:, halves[0]])
        gu_s[:, halves[1]] = _gelu(su1)
        ro1 = _wdot(ret_s[...], wro_ref[:, halves[1]])
        sgu_s[:, halves[0]] = (gu_s[:, halves[0]] * mix_s[:, halves[0]] * _silu(sg0)).astype(_BF16)
        mr0 = proj(_MR0, 0)
        sgu_s[:, halves[1]] = (gu_s[:, halves[1]] * mix_s[:, halves[1]] * _silu(sg1)).astype(_BF16)
        mr1 = proj(_MR0, 1)
        ms0 = proj(_MS0, 0)
        t_s[:, halves[0]] = jax.nn.sigmoid(mr0) * ro0
        ms1 = proj(_MS0, 1)
        t_s[:, halves[1]] = jax.nn.sigmoid(mr1) * ro1
        so0 = _wdot(sgu_s[...], wso_ref[:, halves[0]])
        sm0 = jax.nn.sigmoid(ms0)
        so1 = _wdot(sgu_s[...], wso_ref[:, halves[1]])
        sm1 = jax.nn.sigmoid(ms1)
        mg_s[j, :, halves[0]] = (t_s[:, halves[0]] + sm0 * so0).astype(_BF16)
        mg_s[j, :, halves[1]] = (t_s[:, halves[1]] + sm1 * so1).astype(_BF16)
        return carry

    @pl.when(block < n_blocks)
    def _():
        lax.fori_loop(0, tiles_per_block, tile_body, 0, unroll=True)

    @pl.when(block == n_blocks)
    def _():
        for j in range(tiles_per_block):
            out_only(j)


def _position_tables(seq, ts):
    dk = RET_QK_DIM
    half = dk // 2
    inv = ROPE_BASE ** (-np.arange(half, dtype=np.float64) / half)
    ang = np.arange(seq, dtype=np.float64)[:, None] * inv[None, :]
    cos, sin = np.cos(ang), np.sin(ang)
    cos_full = np.concatenate([cos, cos], axis=-1)
    sin_signed = np.concatenate([-sin, sin], axis=-1)
    scale = dk ** -0.5

    log_g = np.log(1.0 - 2.0 ** (-5.0 - np.arange(RET_HEADS, dtype=np.float64)))
    idx = np.arange(CHUNK, dtype=np.float64)
    diff = idx[:, None] - idx[None, :]
    decay = np.where(diff[None] >= 0,
                     np.exp(np.maximum(diff, 0.0)[None] * log_g[:, None, None]), 0.0)
    zeta = np.exp((CHUNK - 1.0 - idx)[None, :] * log_g[:, None])
    xi = np.exp((idx + 1.0)[None, :] * log_g[:, None])
    cdec = np.exp(CHUNK * log_g)
    reps = ts // CHUNK
    tile = lambda t: np.broadcast_to(np.tile(t, (1, reps))[:, :, None],
                                     (RET_HEADS, ts, dk))
    tables = (cos_full * scale, sin_signed * scale, cos_full, sin_signed,
              tile(xi), tile(zeta), decay,
              np.broadcast_to(cdec[:, None, None], (RET_HEADS, 1, RET_V_DIM)))
    return tuple(jnp.asarray(np.ascontiguousarray(t, dtype=np.float32)) for t in tables)


def _const_spec(shape):
    zeros = (0,) * len(shape)
    return pl.BlockSpec(shape, lambda i: zeros, pipeline_mode=pl.Buffered(1))


@jax.jit
def kernel(x, p, w_in, w_ret_out, w_sgu_out, w_out, sgu_ws, sgu_bs, w_ple_gate,
           w_ple_proj, g_mixer, g_ple, g_final):
    batch, seq, d = x.shape
    depth = w_in.shape[0]
    ts = TOKENS_PER_TILE
    bt = ts * TILES_PER_STEP
    assert depth == 1 and d == D_MODEL and seq % bt == 0 and ts % CHUNK == 0
    n_blocks = batch * seq // bt
    cq, sq, ck, sk, xi, zeta, decay, cdec = _position_tables(seq, ts)
    tril = jnp.tril(jnp.ones((CHUNK, CHUNK), _F32))

    x_blocks = x.reshape(n_blocks, bt, d)
    p_blocks = p[0].reshape(n_blocks, bt, PLE_DIM)
    cur_spec = lambda width: pl.BlockSpec(
        (1, bt, width), lambda b: (jnp.minimum(b, n_blocks - 1), 0, 0))
    lag_spec = lambda width: pl.BlockSpec(
        (1, bt, width), lambda b: (jnp.maximum(b - 1, 0), 0, 0))

    ws_causal = (sgu_ws[0] * tril[None]).astype(_BF16)
    bs_rows = jnp.broadcast_to(sgu_bs[0][:, :, None],
                               (SGU_GROUPS, CHUNK, SGU_GROUP_DIM)).astype(_F32)
    weights = (w_in[0], w_ret_out[0], w_sgu_out[0], w_out[0], w_ple_gate[0], w_ple_proj[0])
    packed = lambda w: pltpu.VMEM((w.shape[0] // 2, w.shape[1]), jnp.uint32)
    blocks_per_seq = seq // bt
    pos_spec = pl.BlockSpec(
        (bt, RET_QK_DIM), lambda b: (lax.rem(jnp.minimum(b, n_blocks - 1), blocks_per_seq), 0))
    consts = (xi, zeta, decay, cdec, ws_causal, bs_rows,
              g_mixer[0][None, :], g_ple[0][None, :], g_final[None, :])
    act = lambda width, dt: pltpu.VMEM((ts, width), dt)
    out = pl.pallas_call(
        functools.partial(_block_kernel, seq // ts, n_blocks),
        grid=(n_blocks + 1,),
        in_specs=[cur_spec(D_MODEL), lag_spec(D_MODEL), lag_spec(PLE_DIM)]
                 + [pos_spec] * 4 + [_const_spec(c.shape) for c in consts]
                 + [pl.BlockSpec(memory_space=pl.ANY)] * len(weights),
        out_specs=lag_spec(D_MODEL),
        out_shape=jax.ShapeDtypeStruct(x_blocks.shape, x.dtype),
        scratch_shapes=[packed(w) for w in weights] + [
            pltpu.VMEM((RET_HEADS, RET_QK_DIM, RET_V_DIM), _F32),
            act(D_MODEL, _BF16),
            act(RET_HEADS * RET_QK_DIM, _BF16),
            act(RET_HEADS * RET_QK_DIM, _BF16),
            act(RET_HEADS * RET_QK_DIM, _BF16),
            act(RET_HEADS * RET_QK_DIM, _BF16),
            act(D_MODEL, _BF16),
            act(D_MODEL, _F32),
            act(D_MODEL, _BF16),
            act(D_MODEL, _F32),
            act(D_MODEL, _BF16),
            act(D_MODEL, _F32),
            act(D_MODEL, _F32),
            act(D_MODEL, _BF16),
            act(D_MODEL, _F32),
            pltpu.VMEM((TILES_PER_STEP, ts, D_MODEL), _BF16),
            act(D_MODEL, _F32),
            act(D_MODEL, _BF16),
            pltpu.VMEM((2, WEIGHT_SLAB_ROWS, WEIGHT_SLAB_COLS), _F32),
            pltpu.SemaphoreType.DMA((2,)),
        ],
        compiler_params=pltpu.CompilerParams(
            dimension_semantics=("arbitrary",),
            vmem_limit_bytes=V7X_VMEM_LIMIT_BYTES),
        name="hybrid_block",
    )(x_blocks, x_blocks, p_blocks, cq, sq, ck, sk, *consts, *weights)
    return out.reshape(x.shape)
```

```python
import functools

import jax
import jax.numpy as jnp
import numpy as np
from jax import lax
from jax.experimental import pallas as pl
from jax.experimental.pallas import tpu as pltpu

D_MODEL = 1024
PLE_DIM = 256
CHUNK = 128
RET_HEADS = 4
RET_V_DIM = D_MODEL // RET_HEADS
RET_QK_DIM = RET_V_DIM // 2
SGU_GROUPS = 4
SGU_GROUP_DIM = D_MODEL // SGU_GROUPS
ROPE_BASE = 10000.0
NORM_EPS = 1e-6
GN_EPS = 1e-5

_QK0 = 0
_V0 = 2 * RET_HEADS * RET_QK_DIM
_RG0 = _V0 + D_MODEL
_SU0 = _RG0 + D_MODEL
_SV0 = _SU0 + D_MODEL
_SG0 = _SV0 + D_MODEL
_MR0 = _SG0 + D_MODEL
_MS0 = _MR0 + D_MODEL
IN_WIDTH = _MS0 + D_MODEL

TOKENS_PER_TILE = 256
TILES_PER_STEP = 2
V7X_VMEM_LIMIT_BYTES = 60 * 1024 * 1024
WEIGHT_SLAB_ROWS = 256
WEIGHT_SLAB_COLS = 2048

_F32 = jnp.float32
_BF16 = jnp.bfloat16


def _dot(a, b):
    return jnp.dot(a, b, preferred_element_type=_F32)


def _wdot(a, w_packed):
    return _dot(a, pltpu.bitcast(w_packed, _BF16))


def _rms(x, g):
    ms = jnp.mean(x * x, axis=-1, keepdims=True)
    return x * lax.rsqrt(ms + NORM_EPS) * g


def _unit(x, eps):
    mu = jnp.mean(x, axis=-1, keepdims=True)
    d = x - mu
    var = jnp.mean(d * d, axis=-1, keepdims=True)
    return d * lax.rsqrt(var + eps)


def _silu(x):
    return x * jax.nn.sigmoid(x)


def _gelu(x):
    return 0.5 * x * (1.0 + lax.erf(x * (0.5 ** 0.5)))


def _block_kernel(tiles_per_seq, n_blocks,
                  x_ref, xl_ref, pl_ref, cq_ref, sq_ref, ck_ref, sk_ref, xi_ref, zeta_ref,
                  decay_ref, cdec_ref, wsc_ref, bs_ref, gm_ref, gp_ref, gf_ref,
                  win_hbm, wro_hbm, wso_hbm, wo_hbm, wpg_hbm, wpp_hbm,
                  o_ref,
                  win_ref, wro_ref, wso_ref, wo_ref, wpg_ref, wpp_ref, stage_ref, stage_sem,
                  state_ref, h_s, q_s, qx_s, k_s, kz_s, v_s, gate_s, ret_s, g_s, vn_s,
                  gu_s, mix_s, sgu_s, t_s, mg_s, x1_s, hp_s):
    ts = TOKENS_PER_TILE
    tiles_per_block = x_ref.shape[1] // ts
    n_chunks = ts // CHUNK
    dk, dv, gd = RET_QK_DIM, RET_V_DIM, SGU_GROUP_DIM
    half = D_MODEL // 2
    halves = (slice(0, half), slice(half, D_MODEL))
    row_blocks = [slice(c * CHUNK, (c + 1) * CHUNK) for c in range(n_chunks)]
    block = pl.program_id(0)

    @pl.when(block == 0)
    def _():
        mg_s[...] = jnp.zeros_like(mg_s)
        slabs = []
        for src_hbm, dst_ref in ((win_hbm, win_ref), (wro_hbm, wro_ref), (wso_hbm, wso_ref),
                                 (wo_hbm, wo_ref), (wpg_hbm, wpg_ref), (wpp_hbm, wpp_ref)):
            k, n = src_hbm.shape
            rows, cols = min(k, WEIGHT_SLAB_ROWS), min(n, WEIGHT_SLAB_COLS)
            for r0 in range(0, k, rows):
                for c0 in range(0, n, cols):
                    slabs.append((src_hbm, dst_ref, r0, c0, rows, cols))

        def slab_copy(i):
            src_hbm, _, r0, c0, rows, cols = slabs[i]
            return pltpu.make_async_copy(
                src_hbm.at[pl.ds(r0, rows), pl.ds(c0, cols)],
                stage_ref.at[i % 2, pl.ds(0, rows), pl.ds(0, cols)],
                stage_sem.at[i % 2])

        slab_copy(0).start()
        for i, (_, dst_ref, r0, c0, rows, cols) in enumerate(slabs):
            if i + 1 < len(slabs):
                slab_copy(i + 1).start()
            slab_copy(i).wait()
            slab = stage_ref[i % 2, 0:rows, 0:cols].astype(_BF16)
            dst_ref[r0 // 2:(r0 + rows) // 2, c0:c0 + cols] = pltpu.bitcast(slab, jnp.uint32)

    def out_only(j):
        rows_j = pl.ds(j * ts, ts)
        for hf in halves:
            x1_s[:, hf] = xl_ref[0, rows_j, hf] + _wdot(mg_s[j], wo_ref[:, hf])
        hp = _rms(x1_s[...], gp_ref[...]).astype(_BF16)
        pg = [_wdot(hp, wpg_ref[:, hf]) for hf in halves]
        pp = _wdot(pl_ref[0, rows_j, :].astype(_BF16), wpp_ref[...])
        x2 = x1_s[...] + jnp.concatenate([jax.nn.sigmoid(g) for g in pg], axis=1) * pp
        o_ref[0, rows_j, :] = _rms(x2, gf_ref[...])

    def tile_body(j, carry):
        tile = block * tiles_per_block + j
        tile_in_seq = lax.rem(tile, tiles_per_seq)
        rows_j = pl.ds(pl.multiple_of(j * ts, ts), ts)

        def proj(col0, hf):
            c0 = col0 + hf * half
            return _wdot(h_s[...], win_ref[:, c0:c0 + half])

        def rotary(val, cos, sin, row_factor_ref, plain_s, scaled_s):
            for hd in range(RET_HEADS):
                lanes = slice(hd * dk, (hd + 1) * dk)
                xh = val[:, lanes]
                xr = xh * cos + pltpu.roll(xh, dk // 2, 1) * sin
                plain_s[:, lanes] = xr.astype(_BF16)
                scaled_s[:, lanes] = (xr * row_factor_ref[hd]).astype(_BF16)

        def head_slices(c, hd):
            return row_blocks[c], slice(hd * dk, (hd + 1) * dk), slice(hd * dv, (hd + 1) * dv)

        def mix_unit(c, g):
            rows = row_blocks[c]
            gl = slice(g * gd, (g + 1) * gd)
            mix_s[rows, gl] = _dot(wsc_ref[g], vn_s[rows, gl]) + bs_ref[g]

        units = [(c, hd) for c in range(n_chunks) for hd in range(RET_HEADS)]


        for hf in halves:
            x1_s[:, hf] = xl_ref[0, rows_j, hf] + _wdot(mg_s[j], wo_ref[:, hf])
        for c, rows in enumerate(row_blocks):
            x_rows = pl.ds(pl.multiple_of(j * ts + c * CHUNK, CHUNK), CHUNK)
            h_s[rows, :] = _rms(x_ref[0, x_rows, :], gm_ref[...]).astype(_BF16)
        q_raw = proj(_QK0, 0)
        for rows in row_blocks:
            hp_s[rows, :] = _rms(x1_s[rows, :], gp_ref[...]).astype(_BF16)
        k_raw = proj(_QK0, 1)
        pg = [_wdot(hp_s[...], wpg_ref[:, hf]) for hf in halves]
        rotary(q_raw, cq_ref[rows_j, :], sq_ref[rows_j, :], xi_ref, q_s, qx_s)
        sv0 = proj(_SV0, 0)
        rotary(k_raw, ck_ref[rows_j, :], sk_ref[rows_j, :], zeta_ref, k_s, kz_s)
        v_s[:, halves[0]] = proj(_V0, 0).astype(_BF16)
        g_s[:, halves[0]] = _gelu(sv0)
        sv1 = proj(_SV0, 1)
        v_s[:, halves[1]] = proj(_V0, 1).astype(_BF16)
        pp = _wdot(pl_ref[0, rows_j, :].astype(_BF16), wpp_ref[...])
        x2 = x1_s[...] + jnp.concatenate([jax.nn.sigmoid(g) for g in pg], axis=1) * pp
        o_ref[0, rows_j, :] = _rms(x2, gf_ref[...])
        rg0 = proj(_RG0, 0)

        scores, kv = {}, {}
        for c, hd in units:
            rows, lanes, vl = head_slices(c, hd)
            scores[c, hd] = lax.dot_general(q_s[rows, lanes], k_s[rows, lanes],
                                            (((1,), (1,)), ((), ())),
                                            preferred_element_type=_F32)
        g_s[:, halves[1]] = _gelu(sv1)
        for c, hd in units:
            rows, lanes, vl = head_slices(c, hd)
            kv[c, hd] = lax.dot_general(kz_s[rows, lanes], v_s[rows, vl],
                                        (((0,), (0,)), ((), ())),
                                        preferred_element_type=_F32)
        rg1 = proj(_RG0, 1)
        gate_s[:, halves[0]] = _silu(rg0)
        prev = {}
        for hd in range(RET_HEADS):
            st = jnp.where(tile_in_seq == 0, 0.0, state_ref[hd])
            for c in range(n_chunks):
                prev[c, hd] = st
                st = kv[c, hd] + cdec_ref[hd] * st
            state_ref[hd] = st
        r = {}
        for c, hd in units:
            rows, lanes, vl = head_slices(c, hd)
            lhs = jnp.concatenate([(scores[c, hd] * decay_ref[hd]).astype(_BF16),
                                   qx_s[rows, lanes]], axis=1)
            rhs = jnp.concatenate([v_s[rows, vl], prev[c, hd].astype(_BF16)], axis=0)
            r[c, hd] = _dot(lhs, rhs)
        gate_s[:, halves[1]] = _silu(rg1)

        su0 = proj(_SU0, 0)
        for rows in row_blocks:
            vn_s[rows, :] = _unit(g_s[rows, :], GN_EPS).astype(_BF16)
        su1 = proj(_SU0, 1)
        for c, hd in units:
            rows, lanes, vl = head_slices(c, hd)
            ret_s[rows, vl] = (_unit(r[c, hd], GN_EPS) * gate_s[rows, vl]).astype(_BF16)
        sg0 = proj(_SG0, 0)
        for c in range(n_chunks):
            for g in range(SGU_GROUPS):
                mix_unit(c, g)
        gu_s[:, halves[0]] = _gelu(su0)
        sg1 = proj(_SG0, 1)
        ro0 = _wdot(ret_s[...], wro_ref[:, halves[0]])
        gu_s[:, halves[1]] = _gelu(su1)
        ro1 = _wdot(ret_s[...], wro_ref[:, halves[1]])
        sgu_s[:, halves[0]] = (gu_s[:, halves[0]] * mix_s[:, halves[0]] * _silu(sg0)).astype(_BF16)
        mr0 = proj(_MR0, 0)
        sgu_s[:, halves[1]] = (gu_s[:, halves[1]] * mix_s[:, halves[1]] * _silu(sg1)).astype(_BF16)
        mr1 = proj(_MR0, 1)
        ms0 = proj(_MS0, 0)
        t_s[:, halves[0]] = jax.nn.sigmoid(mr0) * ro0
        ms1 = proj(_MS0, 1)
        t_s[:, halves[1]] = jax.nn.sigmoid(mr1) * ro1
        so0 = _wdot(sgu_s[...], wso_ref[:, halves[0]])
        sm0 = jax.nn.sigmoid(ms0)
        so1 = _wdot(sgu_s[...], wso_ref[:, halves[1]])
        sm1 = jax.nn.sigmoid(ms1)
        mg_s[j, :, halves[0]] = (t_s[:, halves[0]] + sm0 * so0).astype(_BF16)
        mg_s[j, :, halves[1]] = (t_s[:, halves[1]] + sm1 * so1).astype(_BF16)
        return carry

    @pl.when(block < n_blocks)
    def _():
        lax.fori_loop(0, tiles_per_block, tile_body, 0, unroll=True)

    @pl.when(block == n_blocks)
    def _():
        for j in range(tiles_per_block):
            out_only(j)


def _position_tables(seq, ts):
    dk = RET_QK_DIM
    half = dk // 2
    inv = ROPE_BASE ** (-np.arange(half, dtype=np.float64) / half)
    ang = np.arange(seq, dtype=np.float64)[:, None] * inv[None, :]
    cos, sin = np.cos(ang), np.sin(ang)
    cos_full = np.concatenate([cos, cos], axis=-1)
    sin_signed = np.concatenate([-sin, sin], axis=-1)
    scale = dk ** -0.5

    log_g = np.log(1.0 - 2.0 ** (-5.0 - np.arange(RET_HEADS, dtype=np.float64)))
    idx = np.arange(CHUNK, dtype=np.float64)
    diff = idx[:, None] - idx[None, :]
    decay = np.where(diff[None] >= 0,
                     np.exp(np.maximum(diff, 0.0)[None] * log_g[:, None, None]), 0.0)
    zeta = np.exp((CHUNK - 1.0 - idx)[None, :] * log_g[:, None])
    xi = np.exp((idx + 1.0)[None, :] * log_g[:, None])
    cdec = np.exp(CHUNK * log_g)
    reps = ts // CHUNK
    tile = lambda t: np.broadcast_to(np.tile(t, (1, reps))[:, :, None],
                                     (RET_HEADS, ts, dk))
    tables = (cos_full * scale, sin_signed * scale, cos_full, sin_signed,
              tile(xi), tile(zeta), decay,
              np.broadcast_to(cdec[:, None, None], (RET_HEADS, 1, RET_V_DIM)))
    return tuple(jnp.asarray(np.ascontiguousarray(t, dtype=np.float32)) for t in tables)


def _const_spec(shape):
    zeros = (0,) * len(shape)
    return pl.BlockSpec(shape, lambda i: zeros, pipeline_mode=pl.Buffered(1))


@jax.jit
def kernel(x, p, w_in, w_ret_out, w_sgu_out, w_out, sgu_ws, sgu_bs, w_ple_gate,
           w_ple_proj, g_mixer, g_ple, g_final):
    batch, seq, d = x.shape
    depth = w_in.shape[0]
    ts = TOKENS_PER_TILE
    bt = ts * TILES_PER_STEP
    assert depth == 1 and d == D_MODEL and seq % bt == 0 and ts % CHUNK == 0
    n_blocks = batch * seq // bt
    cq, sq, ck, sk, xi, zeta, decay, cdec = _position_tables(seq, ts)
    tril = jnp.tril(jnp.ones((CHUNK, CHUNK), _F32))

    x_blocks = x.reshape(n_blocks, bt, d)
    p_blocks = p[0].reshape(n_blocks, bt, PLE_DIM)
    cur_spec = lambda width: pl.BlockSpec(
        (1, bt, width), lambda b: (jnp.minimum(b, n_blocks - 1), 0, 0))
    lag_spec = lambda width: pl.BlockSpec(
        (1, bt, width), lambda b: (jnp.maximum(b - 1, 0), 0, 0))

    ws_causal = (sgu_ws[0] * tril[None]).astype(_BF16)
    bs_rows = jnp.broadcast_to(sgu_bs[0][:, :, None],
                               (SGU_GROUPS, CHUNK, SGU_GROUP_DIM)).astype(_F32)
    weights = (w_in[0], w_ret_out[0], w_sgu_out[0], w_out[0], w_ple_gate[0], w_ple_proj[0])
    packed = lambda w: pltpu.VMEM((w.shape[0] // 2, w.shape[1]), jnp.uint32)
    blocks_per_seq = seq // bt
    pos_spec = pl.BlockSpec(
        (bt, RET_QK_DIM), lambda b: (lax.rem(jnp.minimum(b, n_blocks - 1), blocks_per_seq), 0))
    consts = (xi, zeta, decay, cdec, ws_causal, bs_rows,
              g_mixer[0][None, :], g_ple[0][None, :], g_final[None, :])
    act = lambda width, dt: pltpu.VMEM((ts, width), dt)
    out = pl.pallas_call(
        functools.partial(_block_kernel, seq // ts, n_blocks),
        grid=(n_blocks + 1,),
        in_specs=[cur_spec(D_MODEL), lag_spec(D_MODEL), lag_spec(PLE_DIM)]
                 + [pos_spec] * 4 + [_const_spec(c.shape) for c in consts]
                 + [pl.BlockSpec(memory_space=pl.ANY)] * len(weights),
        out_specs=lag_spec(D_MODEL),
        out_shape=jax.ShapeDtypeStruct(x_blocks.shape, x.dtype),
        scratch_shapes=[packed(w) for w in weights] + [
            pltpu.VMEM((2, WEIGHT_SLAB_ROWS, WEIGHT_SLAB_COLS), _F32),
            pltpu.SemaphoreType.DMA((2,)),
            pltpu.VMEM((RET_HEADS, RET_QK_DIM, RET_V_DIM), _F32),
            act(D_MODEL, _BF16),
            act(RET_HEADS * RET_QK_DIM, _BF16),
            act(RET_HEADS * RET_QK_DIM, _BF16),
            act(RET_HEADS * RET_QK_DIM, _BF16),
            act(RET_HEADS * RET_QK_DIM, _BF16),
            act(D_MODEL, _BF16),
            act(D_MODEL, _F32),
            act(D_MODEL, _BF16),
            act(D_MODEL, _F32),
            act(D_MODEL, _BF16),
            act(D_MODEL, _F32),
            act(D_MODEL, _F32),
            act(D_MODEL, _BF16),
            act(D_MODEL, _F32),
            pltpu.VMEM((TILES_PER_STEP, ts, D_MODEL), _BF16),
            act(D_MODEL, _F32),
            act(D_MODEL, _BF16),
        ],
        compiler_params=pltpu.CompilerParams(
            dimension_semantics=("arbitrary",),
            vmem_limit_bytes=V7X_VMEM_LIMIT_BYTES),
        name="hybrid_block",
    )(x_blocks, x_blocks, p_blocks, cq, sq, ck, sk, *consts, *weights)
    return out.reshape(x.shape)
```
